```python
import math
import jax, jax.numpy as jnp
from jax import lax
import numpy as np

D_MODEL = 1024
BATCH = 4
SEQ = 4096
DEPTH = 2

N_A_LAYERS = DEPTH // 2
N_B_LAYERS = DEPTH - N_A_LAYERS
D_FF = ((8 * D_MODEL // 3) + 127) // 128 * 128
CONV_WIDTH = 31
HEAD_DIM = 128
N_HEADS = D_MODEL // HEAD_DIM
MOBA_BLOCK = 256
MOBA_TOPK = 3
Q_CHUNK = 32
RMS_EPS = 1e-6
LN_EPS = 1e-5
MACARON_WEIGHT = 0.5
NEG_INF = -1e30

kernel_name = "yoco_conformer_conv_moba_alibi"


def rms_norm(x, g):
    x32 = x.astype(jnp.float32)
    y = x32 * lax.rsqrt(jnp.mean(x32 * x32, axis=-1, keepdims=True) + RMS_EPS)
    return (y * g.astype(jnp.float32)).astype(x.dtype)


def swiglu_ffn(h, w_gate_up, w_down):
    gate, up = jnp.split(h @ w_gate_up, 2, axis=-1)
    return (jax.nn.silu(gate) * up) @ w_down


def conformer_conv(h, w_in, b_in, w_dw, b_dw, ln_g, ln_b, w_out, b_out):
    a, g = jnp.split(h @ w_in + b_in, 2, axis=-1)
    u = a * jax.nn.sigmoid(g)
    u = lax.conv_general_dilated(
        u, w_dw[:, None, :].astype(u.dtype), window_strides=(1,),
        padding=[(CONV_WIDTH - 1, 0)],
        dimension_numbers=('NWC', 'WIO', 'NWC'),
        feature_group_count=D_MODEL) + b_dw
    u32 = u.astype(jnp.float32)
    mu = jnp.mean(u32, axis=-1, keepdims=True)
    var = jnp.mean(jnp.square(u32 - mu), axis=-1, keepdims=True)
    u = ((u32 - mu) * lax.rsqrt(var + LN_EPS) * ln_g.astype(jnp.float32)
         + ln_b.astype(jnp.float32)).astype(h.dtype)
    return jax.nn.silu(u) @ w_out + b_out


def alibi_slopes():
    return jnp.asarray(2.0 ** (-8.0 * (np.arange(N_HEADS) + 1) / N_HEADS), dtype=jnp.float32)


def shared_kv(x, g_kv, w_kv):
    B, T, _ = x.shape
    k, v = jnp.split(rms_norm(x, g_kv) @ w_kv, 2, axis=-1)
    nb = -(-T // MOBA_BLOCK)
    pad = nb * MOBA_BLOCK - T

    def to_blocks(z):
        z = jnp.pad(z, ((0, 0), (0, pad), (0, 0)))
        return z.reshape(B, nb, MOBA_BLOCK, N_HEADS, HEAD_DIM).transpose(0, 3, 1, 2, 4)

    kb = to_blocks(k)
    vb = to_blocks(v)
    k_means = jnp.mean(kb.astype(jnp.float32), axis=3).astype(kb.dtype)
    return kb, vb, k_means


def moba_attention(q, kb, vb, k_means, slopes):
    B, T = q.shape[0], q.shape[1]
    nb = kb.shape[2]
    ksel = min(MOBA_TOPK, nb)
    q = (q * (HEAD_DIM ** -0.5)).transpose(0, 2, 1, 3)
    qblk = jnp.arange(T) // MOBA_BLOCK
    gate = jnp.einsum('bhtd,bhnd->bhtn', q, k_means, preferred_element_type=jnp.float32)
    past = jnp.arange(nb)[None, :] < qblk[:, None]
    gate = jnp.where(past, gate, NEG_INF)
    _, sel = lax.top_k(gate, ksel)

    nc = T // Q_CHUNK
    q_c = q.reshape(B, N_HEADS, nc, Q_CHUNK, HEAD_DIM).transpose(2, 0, 1, 3, 4)
    sel_c = sel.reshape(B, N_HEADS, nc, Q_CHUNK, ksel).transpose(2, 0, 1, 3, 4)
    b_ix = jnp.arange(B)[:, None, None, None]
    h_ix = jnp.arange(N_HEADS)[None, :, None, None]
    slopes5 = slopes[:, None, None, None]
    slopes4 = slopes[:, None, None]

    def chunk(args):
        c, qq, ss = args
        t = c * Q_CHUNK + jnp.arange(Q_CHUNK)
        own = (c * Q_CHUNK) // MOBA_BLOCK
        valid = jnp.arange(ksel)[None, :] < (t // MOBA_BLOCK)[:, None]
        kg = kb[b_ix, h_ix, ss]
        vg = vb[b_ix, h_ix, ss]
        s_sel = jnp.einsum('bhqd,bhqkpd->bhqkp', qq, kg, preferred_element_type=jnp.float32)
        key_pos = ss[..., None] * MOBA_BLOCK + jnp.arange(MOBA_BLOCK)
        dist = (t[:, None, None] - key_pos).astype(jnp.float32)
        s_sel = jnp.where(valid[:, :, None], s_sel - slopes5 * dist, NEG_INF)
        s_sel = s_sel.reshape(B, N_HEADS, Q_CHUNK, ksel * MOBA_BLOCK)
        k_own = lax.dynamic_index_in_dim(kb, own, axis=2, keepdims=False)
        v_own = lax.dynamic_index_in_dim(vb, own, axis=2, keepdims=False)
        s_own = jnp.einsum('bhqd,bhpd->bhqp', qq, k_own, preferred_element_type=jnp.float32)
        d_own = t[:, None] - (own * MOBA_BLOCK + jnp.arange(MOBA_BLOCK))[None, :]
        s_own = jnp.where(d_own >= 0, s_own - slopes4 * d_own.astype(jnp.float32), NEG_INF)
        p = jax.nn.softmax(jnp.concatenate([s_sel, s_own], axis=-1), axis=-1)
        p_sel = p[..., :ksel * MOBA_BLOCK].reshape(B, N_HEADS, Q_CHUNK, ksel, MOBA_BLOCK).astype(vb.dtype)
        p_own = p[..., ksel * MOBA_BLOCK:].astype(vb.dtype)
        return (jnp.einsum('bhqkp,bhqkpd->bhqd', p_sel, vg)
                + jnp.einsum('bhqp,bhpd->bhqd', p_own, v_own))

    out = lax.map(chunk, (jnp.arange(nc), q_c, sel_c))
    return out.transpose(1, 0, 3, 2, 4).reshape(B, T, N_HEADS * HEAD_DIM)


def setup_inputs(seed: int = 0) -> dict:
    key = jax.random.key(seed)
    ks = jax.random.split(key, 20)
    f32 = jnp.float32

    def nrm(k, shape, scale):
        return jax.random.normal(k, shape, f32) * scale

    def gain(k, shape):
        return 1.0 + 0.02 * jax.random.normal(k, shape, f32)

    D = D_MODEL
    return {
        "x": jax.random.normal(ks[0], (BATCH, SEQ, D), f32),
        "ffn_norm_pre": gain(ks[1], (DEPTH, 2, D)),
        "ffn_norm_post": gain(ks[2], (DEPTH, 2, D)),
        "ffn_w_gate_up": nrm(ks[3], (DEPTH, 2, D, 2 * D_FF), D ** -0.5),
        "ffn_w_down": nrm(ks[4], (DEPTH, 2, D_FF, D), D_FF ** -0.5),
        "mix_norm_pre": gain(ks[5], (DEPTH, D)),
        "mix_norm_post": gain(ks[6], (DEPTH, D)),
        "conv_w_in": nrm(ks[7], (N_A_LAYERS, D, 2 * D), D ** -0.5),
        "conv_b_in": nrm(ks[8], (N_A_LAYERS, 2 * D), 0.02),
        "conv_w_dw": nrm(ks[9], (N_A_LAYERS, CONV_WIDTH, D), CONV_WIDTH ** -0.5),
        "conv_b_dw": nrm(ks[10], (N_A_LAYERS, D), 0.02),
        "conv_ln_g": gain(ks[11], (N_A_LAYERS, D)),
        "conv_ln_b": nrm(ks[12], (N_A_LAYERS, D), 0.02),
        "conv_w_out": nrm(ks[13], (N_A_LAYERS, D, D), D ** -0.5),
        "conv_b_out": nrm(ks[14], (N_A_LAYERS, D), 0.02),
        "kv_norm": gain(ks[15], (D,)),
        "w_kv": nrm(ks[16], (D, 2 * D), D ** -0.5),
        "attn_w_q": nrm(ks[17], (N_B_LAYERS, D, D), D ** -0.5),
        "attn_w_o": nrm(ks[18], (N_B_LAYERS, D, D), D ** -0.5),
    }


def reference(x, ffn_norm_pre, ffn_norm_post, ffn_w_gate_up, ffn_w_down, mix_norm_pre,
              mix_norm_post, conv_w_in, conv_b_in, conv_w_dw, conv_b_dw, conv_ln_g, conv_ln_b,
              conv_w_out, conv_b_out, kv_norm, w_kv, attn_w_q, attn_w_o):
    B, T, _ = x.shape
    slopes = alibi_slopes()
    kv = None
    for layer in range(DEPTH):
        f = swiglu_ffn(rms_norm(x, ffn_norm_pre[layer, 0]), ffn_w_gate_up[layer, 0], ffn_w_down[layer, 0])
        x = x + MACARON_WEIGHT * rms_norm(f, ffn_norm_post[layer, 0])
        h = rms_norm(x, mix_norm_pre[layer])
        if layer < N_A_LAYERS:
            a = layer
            y = conformer_conv(h, conv_w_in[a], conv_b_in[a], conv_w_dw[a], conv_b_dw[a],
                               conv_ln_g[a], conv_ln_b[a], conv_w_out[a], conv_b_out[a])
        else:
            if kv is None:
                kv = shared_kv(x, kv_norm, w_kv)
            j = layer - N_A_LAYERS
            q = (h @ attn_w_q[j]).reshape(B, T, N_HEADS, HEAD_DIM)
            y = moba_attention(q, kv[0], kv[1], kv[2], slopes) @ attn_w_o[j]
        x = x + rms_norm(y, mix_norm_post[layer])
        f = swiglu_ffn(rms_norm(x, ffn_norm_pre[layer, 1]), ffn_w_gate_up[layer, 1], ffn_w_down[layer, 1])
        x = x + MACARON_WEIGHT * rms_norm(f, ffn_norm_post[layer, 1])
    return x
```

```python
import functools

import numpy as np
import jax
import jax.numpy as jnp
from jax import lax
from jax.experimental import pallas as pl
from jax.experimental.pallas import tpu as pltpu

F32 = jnp.float32
BF16 = jnp.bfloat16

RMS_EPS = 1e-6
LN_EPS = 1e-5
MACARON_WEIGHT = 0.5
NEG_INF = -1e30
HEAD_DIM = 128
MOBA_BLOCK = 256
MOBA_TOPK = 3

V7X_VMEM_BYTES = 64 * 1024 * 1024
VMEM_LIMIT_BYTES = 56 * 1024 * 1024
LANES = 128
CONV_HALO = 32

ROW_TILE = 512
FF_CHUNK = 256
HEADS_PER_STEP = 4


def _rms(x, g):
    return x * lax.rsqrt(jnp.mean(x * x, axis=-1, keepdims=True) + RMS_EPS) * g


def _resident(shape):
    return pl.BlockSpec(shape, lambda *_: (0,) * len(shape), pipeline_mode=pl.Buffered(1))


def _params(semantics):
    return pltpu.CompilerParams(dimension_semantics=semantics, vmem_limit_bytes=VMEM_LIMIT_BYTES)


def _ffn_kernel(x_ref, gpre_ref, wgu_ref, wd_ref, gpost_ref, o_ref, *, d_ff, fc):
    x = x_ref[...]
    xn = _rms(x, gpre_ref[...]).astype(BF16)
    acc = jnp.zeros(x.shape, F32)
    for f in range(d_ff // fc):
        gate = jnp.dot(xn, wgu_ref[:, f * fc:(f + 1) * fc], preferred_element_type=F32)
        up = jnp.dot(xn, wgu_ref[:, d_ff + f * fc:d_ff + (f + 1) * fc], preferred_element_type=F32)
        h = (gate * jax.nn.sigmoid(gate) * up).astype(BF16)
        acc = acc + jnp.dot(h, wd_ref[f * fc:(f + 1) * fc, :], preferred_element_type=F32)
    o_ref[...] = x + MACARON_WEIGHT * _rms(acc, gpost_ref[...])


def _ffn(x2d, g_pre, wgu, wd, g_post):
    n, d = x2d.shape
    d_ff = wd.shape[0]
    assert n % ROW_TILE == 0 and d_ff % FF_CHUNK == 0
    row = pl.BlockSpec((ROW_TILE, d), lambda i: (i, 0))
    return pl.pallas_call(
        functools.partial(_ffn_kernel, d_ff=d_ff, fc=FF_CHUNK),
        grid=(n // ROW_TILE,),
        in_specs=[row, _resident((1, d)), _resident((d, 2 * d_ff)), _resident((d_ff, d)), _resident((1, d))],
        out_specs=row,
        out_shape=jax.ShapeDtypeStruct((n, d), F32),
        compiler_params=_params(("parallel",)),
        name="swiglu_ffn",
    )(x2d, g_pre.reshape(1, d), wgu, wd, g_post.reshape(1, d))


def _conv_kernel(x_ref, gpre_ref, win_ref, bin_ref, wdw_ref, bdw_ref, lng_ref, lnb_ref, wout_ref, bout_ref,
                 gpost_ref, o_ref, ubuf, cbuf, *, tm, kw):
    d = x_ref.shape[-1]
    row_chunk = 64

    @pl.when(pl.program_id(1) == 0)
    def _():
        ubuf[0:CONV_HALO, :] = jnp.zeros((CONV_HALO, d), F32)

    x = x_ref[0]
    h = _rms(x, gpre_ref[...]).astype(BF16)
    ag = jnp.dot(h, win_ref[...], preferred_element_type=F32) + bin_ref[...]
    ubuf[CONV_HALO:CONV_HALO + tm, :] = ag[:, :d] * jax.nn.sigmoid(ag[:, d:])

    first = CONV_HALO - (kw - 1)

    def lane_chunk(c, carry):
        cs = pl.ds(pl.multiple_of(c * LANES, LANES), LANES)
        for r0 in range(0, tm, row_chunk):
            acc = jnp.broadcast_to(bdw_ref[:, cs], (row_chunk, LANES))
            for j in range(kw):
                acc = acc + wdw_ref[j:j + 1, cs] * ubuf[r0 + first + j:r0 + first + j + row_chunk, cs]
            cbuf[r0:r0 + row_chunk, cs] = acc
        return carry

    lax.fori_loop(0, d // LANES, lane_chunk, 0)
    ubuf[0:CONV_HALO, :] = ubuf[tm:tm + CONV_HALO, :]

    cv = cbuf[...]
    mu = jnp.mean(cv, axis=-1, keepdims=True)
    cen = cv - mu
    var = jnp.mean(cen * cen, axis=-1, keepdims=True)
    un = cen * lax.rsqrt(var + LN_EPS) * lng_ref[...] + lnb_ref[...]
    s = (un * jax.nn.sigmoid(un)).astype(BF16)
    y = jnp.dot(s, wout_ref[...], preferred_element_type=F32) + bout_ref[...]
    o_ref[0] = x + _rms(y, gpost_ref[...])


def _conv_mixer(x, g_pre, w_in, b_in, w_dw, b_dw, ln_g, ln_b, w_out, b_out, g_post):
    b, t, d = x.shape
    kw = w_dw.shape[0]
    tm = ROW_TILE
    assert t % tm == 0 and kw - 1 <= CONV_HALO <= tm and d % LANES == 0
    row = pl.BlockSpec((1, tm, d), lambda bi, ti: (bi, ti, 0))
    vec = lambda a: a.reshape(1, -1)
    return pl.pallas_call(
        functools.partial(_conv_kernel, tm=tm, kw=kw),
        grid=(b, t // tm),
        in_specs=[row, _resident((1, d)), _resident((d, 2 * d)), _resident((1, 2 * d)), _resident((kw, d)),
                  _resident((1, d)), _resident((1, d)), _resident((1, d)), _resident((d, d)), _resident((1, d)),
                  _resident((1, d))],
        out_specs=row,
        out_shape=jax.ShapeDtypeStruct((b, t, d), F32),
        scratch_shapes=[pltpu.VMEM((CONV_HALO + tm, d), F32), pltpu.VMEM((tm, d), F32)],
        compiler_params=_params(("arbitrary", "arbitrary")),
        name="conformer_conv",
    )(x, vec(g_pre), w_in, vec(b_in), w_dw, vec(b_dw), vec(ln_g), vec(ln_b), w_out, vec(b_out), vec(g_post))


def _qkv_kernel(x_ref, gmix_ref, gkv_ref, wq_ref, wkv_ref, q_ref, k_ref, v_ref, km_ref, *, scale):
    x = x_ref[0]
    tm, d = x.shape
    r = x * lax.rsqrt(jnp.mean(x * x, axis=-1, keepdims=True) + RMS_EPS)
    hq = (r * gmix_ref[...]).astype(BF16)
    hkv = (r * gkv_ref[...]).astype(BF16)
    q_ref[0] = (jnp.dot(hq, wq_ref[...], preferred_element_type=F32) * scale).astype(BF16)
    kv = jnp.dot(hkv, wkv_ref[...], preferred_element_type=F32)
    k = kv[:, :d]
    k_ref[0] = k.astype(BF16)
    v_ref[0] = kv[:, d:].astype(BF16)
    km_ref[0, 0] = jnp.mean(k.reshape(tm // MOBA_BLOCK, MOBA_BLOCK, d), axis=1)


def _qkv(x, g_mix, g_kv, wq, wkv):
    b, t, d = x.shape
    tm = ROW_TILE
    assert t % tm == 0 and tm % MOBA_BLOCK == 0
    per = tm // MOBA_BLOCK
    row = pl.BlockSpec((1, tm, d), lambda bi, ti: (bi, ti, 0))
    q, k, v, km = pl.pallas_call(
        functools.partial(_qkv_kernel, scale=HEAD_DIM ** -0.5),
        grid=(b, t // tm),
        in_specs=[row, _resident((1, d)), _resident((1, d)), _resident((d, d)), _resident((d, 2 * d))],
        out_specs=[row, row, row, pl.BlockSpec((1, 1, per, d), lambda bi, ti: (bi, ti, 0, 0))],
        out_shape=[jax.ShapeDtypeStruct((b, t, d), BF16)] * 3
        + [jax.ShapeDtypeStruct((b, t // tm, per, d), F32)],
        compiler_params=_params(("parallel", "parallel")),
        name="qkv_proj",
    )(x, g_mix.reshape(1, d), g_kv.reshape(1, d), wq, wkv)
    return q, k, v, km.reshape(b, t // MOBA_BLOCK, d)


def _attn_kernel(slopes_ref, q_ref, k_ref, v_ref, km_ref, o_ref, m_s, l_s, acc_s, sd_s, sel_s, *, g):
    blk, dh = MOBA_BLOCK, HEAD_DIM
    hg = pl.program_id(1)
    i = pl.program_id(2)
    nb = km_ref.shape[1]
    nt = (((1,), (1,)), ((), ()))
    tl = lax.broadcasted_iota(jnp.int32, (blk, blk), 0)
    kl = lax.broadcasted_iota(jnp.int32, (blk, blk), 1)
    dist = (tl - kl).astype(F32)
    n_idx = lax.broadcasted_iota(jnp.int32, (1, nb), 1)
    past = n_idx < i

    for hh in range(g):
        hs = slice(hh * dh, (hh + 1) * dh)
        slope = slopes_ref[hg * g + hh]
        q = q_ref[0, :, hs]
        gate = lax.dot_general(q, km_ref[0, :, hs].astype(BF16), nt, preferred_element_type=F32)
        gate = jnp.where(past, gate, NEG_INF)
        rank = jnp.zeros((blk, nb), jnp.int32)
        for m in range(nb):
            gm = gate[:, m:m + 1]
            rank = rank + ((gm > gate) | ((gm == gate) & (m < n_idx))).astype(jnp.int32)
        sel_s[hh] = jnp.where((rank < MOBA_TOPK) & past, 0.0, NEG_INF)
        sd = slope * dist
        sd_s[hh] = sd
        s = lax.dot_general(q, k_ref[0, pl.ds(pl.multiple_of(i * blk, blk), blk), hs], nt,
                            preferred_element_type=F32)
        s = jnp.where(dist >= 0, s - sd, NEG_INF)
        m0 = jnp.max(s, axis=-1, keepdims=True)
        p = jnp.exp(s - m0)
        m_s[hh] = m0
        l_s[hh] = jnp.sum(p, axis=-1, keepdims=True)
        acc_s[hh] = jnp.dot(p.astype(BF16), v_ref[0, pl.ds(pl.multiple_of(i * blk, blk), blk), hs],
                            preferred_element_type=F32)

    def past_block(n, carry):
        rows = pl.ds(pl.multiple_of(n * blk, blk), blk)
        base = ((i - n) * blk).astype(F32)
        for hh in range(g):
            hs = slice(hh * dh, (hh + 1) * dh)
            slope = slopes_ref[hg * g + hh]
            s = lax.dot_general(q_ref[0, :, hs], k_ref[0, rows, hs], nt, preferred_element_type=F32)
            rb = jnp.sum(jnp.where(n_idx == n, sel_s[hh], 0.0), axis=-1, keepdims=True)
            s = s - sd_s[hh] + (rb - slope * base)
            m_old = m_s[hh]
            m_new = jnp.maximum(m_old, jnp.max(s, axis=-1, keepdims=True))
            alpha = jnp.exp(m_old - m_new)
            p = jnp.exp(s - m_new)
            m_s[hh] = m_new
            l_s[hh] = alpha * l_s[hh] + jnp.sum(p, axis=-1, keepdims=True)
            acc_s[hh] = alpha * acc_s[hh] + jnp.dot(p.astype(BF16), v_ref[0, rows, hs],
                                                    preferred_element_type=F32)
        return carry

    lax.fori_loop(0, i, past_block, 0)

    for hh in range(g):
        o_ref[0, :, hh * dh:(hh + 1) * dh] = (acc_s[hh] / l_s[hh]).astype(BF16)


def _attention(q, k, v, km, slopes):
    b, t, d = q.shape
    g, blk = HEADS_PER_STEP, MOBA_BLOCK
    nb = t // blk
    gw = g * HEAD_DIM
    assert t % blk == 0 and d % gw == 0
    qo = pl.BlockSpec((1, blk, gw), lambda bi, hi, ti: (bi, ti, hi))
    kv = pl.BlockSpec((1, t, gw), lambda bi, hi, ti: (bi, 0, hi))
    return pl.pallas_call(
        functools.partial(_attn_kernel, g=g),
        grid=(b, d // gw, nb),
        in_specs=[pl.BlockSpec(memory_space=pltpu.SMEM), qo, kv, kv,
                  pl.BlockSpec((1, nb, gw), lambda bi, hi, ti: (bi, 0, hi))],
        out_specs=qo,
        out_shape=jax.ShapeDtypeStruct((b, t, d), BF16),
        scratch_shapes=[pltpu.VMEM((g, blk, 1), F32), pltpu.VMEM((g, blk, 1), F32),
                        pltpu.VMEM((g, blk, HEAD_DIM), F32), pltpu.VMEM((g, blk, blk), F32),
                        pltpu.VMEM((g, blk, nb), F32)],
        compiler_params=_params(("parallel", "parallel", "arbitrary")),
        name="moba_attention",
    )(slopes, q, k, v, km)


def _oproj_kernel(x_ref, a_ref, wo_ref, gpost_ref, o_ref):
    y = jnp.dot(a_ref[...], wo_ref[...], preferred_element_type=F32)
    o_ref[...] = x_ref[...] + _rms(y, gpost_ref[...])


def _oproj(x2d, a2d, wo, g_post):
    n, d = x2d.shape
    row = pl.BlockSpec((ROW_TILE, d), lambda i: (i, 0))
    return pl.pallas_call(
        _oproj_kernel,
        grid=(n // ROW_TILE,),
        in_specs=[row, row, _resident((d, d)), _resident((1, d))],
        out_specs=row,
        out_shape=jax.ShapeDtypeStruct((n, d), F32),
        compiler_params=_params(("parallel",)),
        name="attn_out_proj",
    )(x2d, a2d, wo, g_post.reshape(1, d))


def kernel(x, ffn_norm_pre, ffn_norm_post, ffn_w_gate_up, ffn_w_down, mix_norm_pre, mix_norm_post, conv_w_in,
           conv_b_in, conv_w_dw, conv_b_dw, conv_ln_g, conv_ln_b, conv_w_out, conv_b_out, kv_norm, w_kv,
           attn_w_q, attn_w_o):
    b, t, d = x.shape
    depth = ffn_w_gate_up.shape[0]
    n_conv = conv_w_in.shape[0]
    assert depth - n_conv == attn_w_q.shape[0] == 1, "one MoBA layer reading the shared K/V is supported"
    n_heads = d // HEAD_DIM
    slopes = jnp.asarray(2.0 ** (-8.0 * (np.arange(n_heads) + 1) / n_heads), dtype=F32)
    bf = lambda w: w.astype(BF16)

    for layer in range(depth):
        x = _ffn(x.reshape(b * t, d), ffn_norm_pre[layer, 0], bf(ffn_w_gate_up[layer, 0]),
                 bf(ffn_w_down[layer, 0]), ffn_norm_post[layer, 0]).reshape(b, t, d)
        if layer < n_conv:
            a = layer
            x = _conv_mixer(x, mix_norm_pre[layer], bf(conv_w_in[a]), conv_b_in[a], conv_w_dw[a], conv_b_dw[a],
                            conv_ln_g[a], conv_ln_b[a], bf(conv_w_out[a]), conv_b_out[a], mix_norm_post[layer])
        else:
            j = layer - n_conv
            q, k, v, km = _qkv(x, mix_norm_pre[layer], kv_norm, bf(attn_w_q[j]), bf(w_kv))
            att = _attention(q, k, v, km, slopes)
            x = _oproj(x.reshape(b * t, d), att.reshape(b * t, d), bf(attn_w_o[j]),
                       mix_norm_post[layer]).reshape(b, t, d)
        x = _ffn(x.reshape(b * t, d), ffn_norm_pre[layer, 1], bf(ffn_w_gate_up[layer, 1]),
                 bf(ffn_w_down[layer, 1]), ffn_norm_post[layer, 1]).reshape(b, t, d)
    return x
```

```python
import functools

import numpy as np
import jax
import jax.numpy as jnp
from jax import lax
from jax.experimental import pallas as pl
from jax.experimental.pallas import tpu as pltpu

F32 = jnp.float32
BF16 = jnp.bfloat16

RMS_EPS = 1e-6
LN_EPS = 1e-5
MACARON_WEIGHT = 0.5
NEG_INF = -1e30
HEAD_DIM = 128
MOBA_BLOCK = 256
MOBA_TOPK = 3

V7X_VMEM_BYTES = 64 * 1024 * 1024
VMEM_LIMIT_BYTES = 56 * 1024 * 1024
LANES = 128
SUBLANES = 8
CONV_HALO = 32

ROW_TILE = 512
FF_CHUNK = 256
HEADS_PER_STEP = 8


_NT = (((1,), (1,)), ((), ()))


def _rms(x, g):
    return x * lax.rsqrt(jnp.mean(x * x, axis=-1, keepdims=True) + RMS_EPS) * g


def _resident(shape):
    return pl.BlockSpec(shape, lambda *_: (0,) * len(shape), pipeline_mode=pl.Buffered(1))


def _params(semantics):
    return pltpu.CompilerParams(dimension_semantics=semantics, vmem_limit_bytes=VMEM_LIMIT_BYTES)


def _ffn_kernel(x_ref, gpre_ref, wgu_ref, wd_ref, gpost_ref, o_ref, *, d_ff, fc):
    x = x_ref[...]
    xn = _rms(x, gpre_ref[...]).astype(BF16)
    acc = jnp.zeros(x.shape, F32)
    for f in range(d_ff // fc):
        gate = jnp.dot(xn, wgu_ref[:, f * fc:(f + 1) * fc], preferred_element_type=F32)
        up = jnp.dot(xn, wgu_ref[:, d_ff + f * fc:d_ff + (f + 1) * fc], preferred_element_type=F32)
        h = (gate * jax.nn.sigmoid(gate) * up).astype(BF16)
        acc = acc + jnp.dot(h, wd_ref[f * fc:(f + 1) * fc, :], preferred_element_type=F32)
    o_ref[...] = x + MACARON_WEIGHT * _rms(acc, gpost_ref[...])


def _ffn(x2d, g_pre, wgu, wd, g_post):
    n, d = x2d.shape
    d_ff = wd.shape[0]
    assert n % ROW_TILE == 0 and d_ff % FF_CHUNK == 0
    row = pl.BlockSpec((ROW_TILE, d), lambda i: (i, 0))
    return pl.pallas_call(
        functools.partial(_ffn_kernel, d_ff=d_ff, fc=FF_CHUNK),
        grid=(n // ROW_TILE,),
        in_specs=[row, _resident((1, d)), _resident((d, 2 * d_ff)), _resident((d_ff, d)), _resident((1, d))],
        out_specs=row,
        out_shape=jax.ShapeDtypeStruct((n, d), F32),
        compiler_params=_params(("parallel",)),
        name="swiglu_ffn",
    )(x2d, g_pre.reshape(1, d), wgu, wd, g_post.reshape(1, d))


def _conv_kernel(x_ref, gpre_ref, win_ref, bin_ref, wdw_ref, bdw_ref, lng_ref, lnb_ref, wout_ref, bout_ref,
                 gpost_ref, o_ref, ubuf, cbuf, *, tm, kw):
    d = x_ref.shape[-1]
    row_chunk = 64

    @pl.when(pl.program_id(1) == 0)
    def _():
        ubuf[0:CONV_HALO, :] = jnp.zeros((CONV_HALO, d), F32)

    x = x_ref[0]
    h = _rms(x, gpre_ref[...]).astype(BF16)
    ag = jnp.dot(h, win_ref[...], preferred_element_type=F32) + bin_ref[...]
    ubuf[CONV_HALO:CONV_HALO + tm, :] = ag[:, :d] * jax.nn.sigmoid(ag[:, d:])

    first = CONV_HALO - (kw - 1)
    taps_by_shift = {}
    for j in range(kw):
        shift = (first + j) % SUBLANES
        taps_by_shift.setdefault(shift, []).append((j, first + j - shift))

    def lane_chunk(c, carry):
        cs = pl.ds(pl.multiple_of(c * LANES, LANES), LANES)
        for r0 in range(0, tm, row_chunk):
            win = ubuf[r0:r0 + row_chunk + CONV_HALO, cs]
            acc = jnp.broadcast_to(bdw_ref[:, cs], (row_chunk, LANES))
            for shift, taps in sorted(taps_by_shift.items()):
                rows = row_chunk + (SUBLANES if shift else 0)
                part = None
                for j, start in taps:
                    term = wdw_ref[j:j + 1, cs] * win[start:start + rows]
                    part = term if part is None else part + term
                acc = acc + part[shift:shift + row_chunk]
            cbuf[r0:r0 + row_chunk, cs] = acc
        return carry

    lax.fori_loop(0, d // LANES, lane_chunk, 0)
    ubuf[0:CONV_HALO, :] = ubuf[tm:tm + CONV_HALO, :]

    cv = cbuf[...]
    mu = jnp.mean(cv, axis=-1, keepdims=True)
    cen = cv - mu
    var = jnp.mean(cen * cen, axis=-1, keepdims=True)
    un = cen * lax.rsqrt(var + LN_EPS) * lng_ref[...] + lnb_ref[...]
    s = (un * jax.nn.sigmoid(un)).astype(BF16)
    y = jnp.dot(s, wout_ref[...], preferred_element_type=F32) + bout_ref[...]
    o_ref[0] = x + _rms(y, gpost_ref[...])


def _conv_mixer(x, g_pre, w_in, b_in, w_dw, b_dw, ln_g, ln_b, w_out, b_out, g_post):
    b, t, d = x.shape
    kw = w_dw.shape[0]
    tm = ROW_TILE
    assert t % tm == 0 and kw - 1 <= CONV_HALO <= tm and d % LANES == 0
    row = pl.BlockSpec((1, tm, d), lambda bi, ti: (bi, ti, 0))
    vec = lambda a: a.reshape(1, -1)
    return pl.pallas_call(
        functools.partial(_conv_kernel, tm=tm, kw=kw),
        grid=(b, t // tm),
        in_specs=[row, _resident((1, d)), _resident((d, 2 * d)), _resident((1, 2 * d)), _resident((kw, d)),
                  _resident((1, d)), _resident((1, d)), _resident((1, d)), _resident((d, d)), _resident((1, d)),
                  _resident((1, d))],
        out_specs=row,
        out_shape=jax.ShapeDtypeStruct((b, t, d), F32),
        scratch_shapes=[pltpu.VMEM((CONV_HALO + tm, d), F32), pltpu.VMEM((tm, d), F32)],
        compiler_params=_params(("arbitrary", "arbitrary")),
        name="conformer_conv",
    )(x, vec(g_pre), w_in, vec(b_in), w_dw, vec(b_dw), vec(ln_g), vec(ln_b), w_out, vec(b_out), vec(g_post))


def _qkv_kernel(x_ref, gmix_ref, gkv_ref, wq_ref, wk_ref, wvt_ref, q_ref, k_ref, vt_ref, km_ref, *, scale):
    x = x_ref[0]
    tm, d = x.shape
    r = x * lax.rsqrt(jnp.mean(x * x, axis=-1, keepdims=True) + RMS_EPS)
    hq = (r * gmix_ref[...]).astype(BF16)
    hkv = (r * gkv_ref[...]).astype(BF16)
    q_ref[0] = (jnp.dot(hq, wq_ref[...], preferred_element_type=F32) * scale).astype(BF16)
    k = jnp.dot(hkv, wk_ref[...], preferred_element_type=F32)
    k_ref[0] = k.astype(BF16)
    km_ref[0, 0] = jnp.mean(k.reshape(tm // MOBA_BLOCK, MOBA_BLOCK, d), axis=1)
    vt_ref[0] = lax.dot_general(wvt_ref[...], hkv, _NT, preferred_element_type=F32).astype(BF16)


def _qkv(x, g_mix, g_kv, wq, wk, wvt):
    b, t, d = x.shape
    tm = ROW_TILE
    assert t % tm == 0 and tm % MOBA_BLOCK == 0
    per = tm // MOBA_BLOCK
    row = pl.BlockSpec((1, tm, d), lambda bi, ti: (bi, ti, 0))
    q, k, vt, km = pl.pallas_call(
        functools.partial(_qkv_kernel, scale=HEAD_DIM ** -0.5),
        grid=(b, t // tm),
        in_specs=[row, _resident((1, d)), _resident((1, d)), _resident((d, d)), _resident((d, d)),
                  _resident((d, d))],
        out_specs=[row, row, pl.BlockSpec((1, d, tm), lambda bi, ti: (bi, 0, ti)),
                   pl.BlockSpec((1, 1, per, d), lambda bi, ti: (bi, ti, 0, 0))],
        out_shape=[jax.ShapeDtypeStruct((b, t, d), BF16), jax.ShapeDtypeStruct((b, t, d), BF16),
                   jax.ShapeDtypeStruct((b, d, t), BF16), jax.ShapeDtypeStruct((b, t // tm, per, d), F32)],
        compiler_params=_params(("parallel", "parallel")),
        name="qkv_proj",
    )(x, g_mix.reshape(1, d), g_kv.reshape(1, d), wq, wk, wvt)
    return q, k, vt, km.reshape(b, t // MOBA_BLOCK, d)


def _attn_kernel(slopes_ref, q_ref, k_ref, vt_ref, km_ref, o_ref, m_s, l_s, acc_s, sd_s, sel_s, *, g):
    blk, dh = MOBA_BLOCK, HEAD_DIM
    hg = pl.program_id(1)
    i = pl.program_id(2)
    nb = km_ref.shape[1]
    key_l = lax.broadcasted_iota(jnp.int32, (blk, blk), 0)
    qry_l = lax.broadcasted_iota(jnp.int32, (blk, blk), 1)
    dist = (qry_l - key_l).astype(F32)
    n_idx = lax.broadcasted_iota(jnp.int32, (nb, blk), 0)
    past = n_idx < i
    own = pl.ds(pl.multiple_of(i * blk, blk), blk)

    heads = [slice(hh * dh, (hh + 1) * dh) for hh in range(g)]
    slopes = [slopes_ref[hg * g + hh] for hh in range(g)]

    own_scores = [lax.dot_general(k_ref[0, own, hs], q_ref[0, :, hs], _NT, preferred_element_type=F32)
                  for hs in heads]
    own_values = [vt_ref[0, hs, own] for hs in heads]
    gates = [lax.dot_general(km_ref[0, :, hs].astype(BF16), q_ref[0, :, hs], _NT, preferred_element_type=F32)
             for hs in heads]
    init = []
    for hh in range(g):
        gate = jnp.where(past, gates[hh], NEG_INF)
        rank = jnp.zeros((nb, blk), jnp.int32)
        for m in range(nb):
            gm = gate[m:m + 1, :]
            rank = rank + ((gm > gate) | ((gm == gate) & (m < n_idx))).astype(jnp.int32)
        sel = jnp.where((rank < MOBA_TOPK) & past, 0.0, NEG_INF)
        sd = slopes[hh] * dist
        s = jnp.where(dist >= 0, own_scores[hh] - sd, NEG_INF)
        m0 = jnp.max(s, axis=0, keepdims=True)
        p = jnp.exp(s - m0)
        acc0 = jnp.dot(own_values[hh], p.astype(BF16), preferred_element_type=F32)
        init.append((sel, sd, m0, jnp.sum(p, axis=0, keepdims=True), acc0))
    for hh in range(g):
        sel_s[hh], sd_s[hh], m_s[hh], l_s[hh], acc_s[hh] = init[hh]

    def past_block(n, carry):
        rows = pl.ds(pl.multiple_of(n * blk, blk), blk)
        base = ((i - n) * blk).astype(F32)
        scores = [lax.dot_general(k_ref[0, rows, hs], q_ref[0, :, hs], _NT, preferred_element_type=F32)
                  for hs in heads]
        values = [vt_ref[0, hs, rows] for hs in heads]
        new = []
        for hh in range(g):
            qbias = sel_s[hh, pl.ds(n, 1), :] - slopes[hh] * base
            u = scores[hh] - sd_s[hh]
            m_old = m_s[hh]
            m_new = jnp.maximum(m_old, jnp.max(u, axis=0, keepdims=True) + qbias)
            alpha = jnp.exp(m_old - m_new)
            p = jnp.exp(u - (m_new - qbias))
            new.append((m_new, alpha * l_s[hh] + jnp.sum(p, axis=0, keepdims=True),
                        alpha * acc_s[hh] + jnp.dot(values[hh], p.astype(BF16), preferred_element_type=F32)))
        for hh in range(g):
            m_s[hh], l_s[hh], acc_s[hh] = new[hh]
        return carry

    lax.fori_loop(0, i, past_block, 0)

    for hh in range(g):
        o_ref[0, :, hh * dh:(hh + 1) * dh] = (acc_s[hh] / l_s[hh]).T.astype(BF16)


def _attention(q, k, vt, km, slopes):
    b, t, d = q.shape
    g, blk = HEADS_PER_STEP, MOBA_BLOCK
    nb = t // blk
    gw = g * HEAD_DIM
    assert t % blk == 0 and d % gw == 0
    qo = pl.BlockSpec((1, blk, gw), lambda bi, hi, ti: (bi, ti, hi))
    return pl.pallas_call(
        functools.partial(_attn_kernel, g=g),
        grid=(b, d // gw, nb),
        in_specs=[pl.BlockSpec(memory_space=pltpu.SMEM), qo,
                  pl.BlockSpec((1, t, gw), lambda bi, hi, ti: (bi, 0, hi)),
                  pl.BlockSpec((1, gw, t), lambda bi, hi, ti: (bi, hi, 0)),
                  pl.BlockSpec((1, nb, gw), lambda bi, hi, ti: (bi, 0, hi))],
        out_specs=qo,
        out_shape=jax.ShapeDtypeStruct((b, t, d), BF16),
        scratch_shapes=[pltpu.VMEM((g, 1, blk), F32), pltpu.VMEM((g, 1, blk), F32),
                        pltpu.VMEM((g, HEAD_DIM, blk), F32), pltpu.VMEM((g, blk, blk), F32),
                        pltpu.VMEM((g, nb, blk), F32)],
        compiler_params=_params(("parallel", "parallel", "arbitrary")),
        name="moba_attention",
    )(slopes, q, k, vt, km)


def _oproj_kernel(x_ref, a_ref, wo_ref, gpost_ref, o_ref):
    y = jnp.dot(a_ref[...], wo_ref[...], preferred_element_type=F32)
    o_ref[...] = x_ref[...] + _rms(y, gpost_ref[...])


def _oproj(x2d, a2d, wo, g_post):
    n, d = x2d.shape
    row = pl.BlockSpec((ROW_TILE, d), lambda i: (i, 0))
    return pl.pallas_call(
        _oproj_kernel,
        grid=(n // ROW_TILE,),
        in_specs=[row, row, _resident((d, d)), _resident((1, d))],
        out_specs=row,
        out_shape=jax.ShapeDtypeStruct((n, d), F32),
        compiler_params=_params(("parallel",)),
        name="attn_out_proj",
    )(x2d, a2d, wo, g_post.reshape(1, d))


def kernel(x, ffn_norm_pre, ffn_norm_post, ffn_w_gate_up, ffn_w_down, mix_norm_pre, mix_norm_post, conv_w_in,
           conv_b_in, conv_w_dw, conv_b_dw, conv_ln_g, conv_ln_b, conv_w_out, conv_b_out, kv_norm, w_kv,
           attn_w_q, attn_w_o):
    b, t, d = x.shape
    depth = ffn_w_gate_up.shape[0]
    n_conv = conv_w_in.shape[0]
    assert depth - n_conv == attn_w_q.shape[0] == 1, "one MoBA layer reading the shared K/V is supported"
    n_heads = d // HEAD_DIM
    slopes = jnp.asarray(2.0 ** (-8.0 * (np.arange(n_heads) + 1) / n_heads), dtype=F32)
    bf = lambda w: w.astype(BF16)

    for layer in range(depth):
        x = _ffn(x.reshape(b * t, d), ffn_norm_pre[layer, 0], bf(ffn_w_gate_up[layer, 0]),
                 bf(ffn_w_down[layer, 0]), ffn_norm_post[layer, 0]).reshape(b, t, d)
        if layer < n_conv:
            a = layer
            x = _conv_mixer(x, mix_norm_pre[layer], bf(conv_w_in[a]), conv_b_in[a], conv_w_dw[a], conv_b_dw[a],
                            conv_ln_g[a], conv_ln_b[a], bf(conv_w_out[a]), conv_b_out[a], mix_norm_post[layer])
        else:
            j = layer - n_conv
            q, k, vt, km = _qkv(x, mix_norm_pre[layer], kv_norm, bf(attn_w_q[j]), bf(w_kv[:, :d]),
                                bf(w_kv[:, d:].T))
            att = _attention(q, k, vt, km, slopes)
            x = _oproj(x.reshape(b * t, d), att.reshape(b * t, d), bf(attn_w_o[j]),
                       mix_norm_post[layer]).reshape(b, t, d)
        x = _ffn(x.reshape(b * t, d), ffn_norm_pre[layer, 1], bf(ffn_w_gate_up[layer, 1]),
                 bf(ffn_w_down[layer, 1]), ffn_norm_post[layer, 1]).reshape(b, t, d)
    return x
```

```python
import functools

import numpy as np
import jax
import jax.numpy as jnp
from jax import lax
from jax.experimental import pallas as pl
from jax.experimental.pallas import tpu as pltpu

F32 = jnp.float32
BF16 = jnp.bfloat16

RMS_EPS = 1e-6
LN_EPS = 1e-5
MACARON_WEIGHT = 0.5
NEG_INF = -1e30
LOG2_E = 1.4426950408889634
HEAD_DIM = 128
MOBA_BLOCK = 256
MOBA_TOPK = 3

V7X_VMEM_BYTES = 64 * 1024 * 1024
VMEM_LIMIT_BYTES = 56 * 1024 * 1024
LANES = 128
SUBLANES = 8
BF16_SUBLANES = 16
CONV_HALO = 32

ROW_TILE = 512
FF_CHUNK = 256
HEADS_PER_STEP = 8


_NT = (((1,), (1,)), ((), ()))


def _rms(x, g):
    return x * lax.rsqrt(jnp.mean(x * x, axis=-1, keepdims=True) + RMS_EPS) * g


def _resident(shape):
    return pl.BlockSpec(shape, lambda *_: (0,) * len(shape), pipeline_mode=pl.Buffered(1))


def _params(semantics):
    return pltpu.CompilerParams(dimension_semantics=semantics, vmem_limit_bytes=VMEM_LIMIT_BYTES)


def _ffn_kernel(x_ref, gpre_ref, wgu_ref, wd_ref, gpost_ref, o_ref, *, d_ff, fc):
    x = x_ref[...]
    xn = _rms(x, gpre_ref[...]).astype(BF16)
    acc = jnp.zeros(x.shape, F32)
    for f in range(d_ff // fc):
        gate = jnp.dot(xn, wgu_ref[:, f * fc:(f + 1) * fc], preferred_element_type=F32)
        up = jnp.dot(xn, wgu_ref[:, d_ff + f * fc:d_ff + (f + 1) * fc], preferred_element_type=F32)
        h = (gate * jax.nn.sigmoid(gate) * up).astype(BF16)
        acc = acc + jnp.dot(h, wd_ref[f * fc:(f + 1) * fc, :], preferred_element_type=F32)
    o_ref[...] = x + MACARON_WEIGHT * _rms(acc, gpost_ref[...])


def _ffn(x2d, g_pre, wgu_all, wd_all, g_post, layer, half):
    n, d = x2d.shape
    d_ff = wd_all.shape[2]
    assert n % ROW_TILE == 0 and d_ff % FF_CHUNK == 0
    row = pl.BlockSpec((ROW_TILE, d), lambda i: (i, 0))
    pick = lambda i: (layer, half, 0, 0)
    return pl.pallas_call(
        functools.partial(_ffn_kernel, d_ff=d_ff, fc=FF_CHUNK),
        grid=(n // ROW_TILE,),
        in_specs=[row, _resident((1, d)),
                  pl.BlockSpec((None, None, d, 2 * d_ff), pick, pipeline_mode=pl.Buffered(1)),
                  pl.BlockSpec((None, None, d_ff, d), pick, pipeline_mode=pl.Buffered(1)), _resident((1, d))],
        out_specs=row,
        out_shape=jax.ShapeDtypeStruct((n, d), F32),
        compiler_params=_params(("parallel",)),
        name="swiglu_ffn",
    )(x2d, g_pre[layer, half].reshape(1, d), wgu_all, wd_all, g_post[layer, half].reshape(1, d))


def _conv_kernel(x_ref, gpre_ref, win_ref, bin_ref, wdw_ref, bdw_ref, lng_ref, lnb_ref, wout_ref, bout_ref,
                 gpost_ref, o_ref, ubuf, cbuf, *, tm, kw):
    d = x_ref.shape[-1]
    row_chunk = 64

    @pl.when(pl.program_id(1) == 0)
    def _():
        ubuf[0:CONV_HALO, :] = jnp.zeros((CONV_HALO, d), F32)

    x = x_ref[0]
    h = _rms(x, gpre_ref[...]).astype(BF16)
    ag = jnp.dot(h, win_ref[...], preferred_element_type=F32) + bin_ref[...]
    ubuf[CONV_HALO:CONV_HALO + tm, :] = ag[:, :d] * jax.nn.sigmoid(ag[:, d:])

    first = CONV_HALO - (kw - 1)
    taps_by_shift = {}
    for j in range(kw):
        shift = (first + j) % SUBLANES
        taps_by_shift.setdefault(shift, []).append((j, first + j - shift))

    def lane_chunk(c, carry):
        cs = pl.ds(pl.multiple_of(c * LANES, LANES), LANES)
        for r0 in range(0, tm, row_chunk):
            win = ubuf[r0:r0 + row_chunk + CONV_HALO, cs]
            acc = jnp.broadcast_to(bdw_ref[:, cs], (row_chunk, LANES))
            for shift, taps in sorted(taps_by_shift.items()):
                rows = row_chunk + (SUBLANES if shift else 0)
                part = None
                for j, start in taps:
                    term = wdw_ref[j:j + 1, cs] * win[start:start + rows]
                    part = term if part is None else part + term
                acc = acc + part[shift:shift + row_chunk]
            cbuf[r0:r0 + row_chunk, cs] = acc
        return carry

    lax.fori_loop(0, d // LANES, lane_chunk, 0)
    ubuf[0:CONV_HALO, :] = ubuf[tm:tm + CONV_HALO, :]

    cv = cbuf[...]
    mu = jnp.mean(cv, axis=-1, keepdims=True)
    cen = cv - mu
    var = jnp.mean(cen * cen, axis=-1, keepdims=True)
    un = cen * lax.rsqrt(var + LN_EPS) * lng_ref[...] + lnb_ref[...]
    s = (un * jax.nn.sigmoid(un)).astype(BF16)
    y = jnp.dot(s, wout_ref[...], preferred_element_type=F32) + bout_ref[...]
    o_ref[0] = x + _rms(y, gpost_ref[...])


def _conv_mixer(x, g_pre, w_in, b_in, w_dw, b_dw, ln_g, ln_b, w_out, b_out, g_post):
    b, t, d = x.shape
    kw = w_dw.shape[0]
    tm = ROW_TILE
    assert t % tm == 0 and kw - 1 <= CONV_HALO <= tm and d % LANES == 0
    row = pl.BlockSpec((1, tm, d), lambda bi, ti: (bi, ti, 0))
    vec = lambda a: a.reshape(1, -1)
    return pl.pallas_call(
        functools.partial(_conv_kernel, tm=tm, kw=kw),
        grid=(b, t // tm),
        in_specs=[row, _resident((1, d)), _resident((d, 2 * d)), _resident((1, 2 * d)), _resident((kw, d)),
                  _resident((1, d)), _resident((1, d)), _resident((1, d)), _resident((d, d)), _resident((1, d)),
                  _resident((1, d))],
        out_specs=row,
        out_shape=jax.ShapeDtypeStruct((b, t, d), F32),
        scratch_shapes=[pltpu.VMEM((CONV_HALO + tm, d), F32), pltpu.VMEM((tm, d), F32)],
        compiler_params=_params(("arbitrary", "arbitrary")),
        name="conformer_conv",
    )(x, vec(g_pre), w_in, vec(b_in), w_dw, vec(b_dw), vec(ln_g), vec(ln_b), w_out, vec(b_out), vec(g_post))


def _qkv_kernel(x_ref, gmix_ref, gkv_ref, wq_ref, wk_ref, wvt_ref, q_ref, k_ref, vt_ref, km_ref, *, scale):
    x = x_ref[0]
    tm, d = x.shape
    r = x * lax.rsqrt(jnp.mean(x * x, axis=-1, keepdims=True) + RMS_EPS)
    hq = (r * gmix_ref[...]).astype(BF16)
    hkv = (r * gkv_ref[...]).astype(BF16)
    q_ref[0] = (jnp.dot(hq, wq_ref[...], preferred_element_type=F32) * scale).astype(BF16)
    k = jnp.dot(hkv, wk_ref[...], preferred_element_type=F32)
    k_ref[0] = k.astype(BF16)
    km_ref[0, 0] = jnp.mean(k.reshape(tm // MOBA_BLOCK, MOBA_BLOCK, d), axis=1)
    vt_ref[0] = lax.dot_general(wvt_ref[...], hkv, _NT, preferred_element_type=F32).astype(BF16)


def _qkv(x, g_mix, g_kv, wq, wk, wvt):
    b, t, d = x.shape
    tm = ROW_TILE
    assert t % tm == 0 and tm % MOBA_BLOCK == 0
    per = tm // MOBA_BLOCK
    row = pl.BlockSpec((1, tm, d), lambda bi, ti: (bi, ti, 0))
    q, k, vt, km = pl.pallas_call(
        functools.partial(_qkv_kernel, scale=HEAD_DIM ** -0.5 * LOG2_E),
        grid=(b, t // tm),
        in_specs=[row, _resident((1, d)), _resident((1, d)), _resident((d, d)), _resident((d, d)),
                  _resident((d, d))],
        out_specs=[row, row, pl.BlockSpec((1, d, tm), lambda bi, ti: (bi, 0, ti)),
                   pl.BlockSpec((1, 1, per, d), lambda bi, ti: (bi, ti, 0, 0))],
        out_shape=[jax.ShapeDtypeStruct((b, t, d), BF16), jax.ShapeDtypeStruct((b, t, d), BF16),
                   jax.ShapeDtypeStruct((b, d, t), BF16), jax.ShapeDtypeStruct((b, t // tm, per, d), F32)],
        compiler_params=_params(("parallel", "parallel")),
        name="qkv_proj",
    )(x, g_mix.reshape(1, d), g_kv.reshape(1, d), wq, wk, wvt)
    return q, k, vt, km.reshape(b, t // MOBA_BLOCK, d)


def _select_topk(gate, n_idx):
    picked = jnp.zeros(gate.shape, jnp.bool_)
    for _ in range(MOBA_TOPK):
        best = jnp.max(gate, axis=0, keepdims=True)
        first = jnp.min(jnp.where(gate == best, n_idx, gate.shape[0]), axis=0, keepdims=True)
        hit = n_idx == first
        picked = picked | hit
        gate = jnp.where(hit, -jnp.inf, gate)
    return picked


def _attn_kernel(slopes_ref, q_ref, k_ref, vt_ref, km_ref, sd_ref, own_ref, o_ref, m_s, l_s, mx_s, acc_s, sel_s,
                 sc_s, *, g):
    blk, dh = MOBA_BLOCK, HEAD_DIM
    hg = pl.program_id(1)
    i = pl.program_id(2)
    nb = km_ref.shape[1]
    n_idx = lax.broadcasted_iota(jnp.int32, (nb, blk), 0)
    past = n_idx < i
    ones_rows = jnp.ones((BF16_SUBLANES, blk), BF16)

    heads = [slice(hh * dh, (hh + 1) * dh) for hh in range(g)]
    slopes = [slopes_ref[hg * g + hh] for hh in range(g)]

    def block_rows(n):
        return pl.ds(pl.multiple_of(n * blk, blk), blk)

    def scores_of(rows):
        return [lax.dot_general(k_ref[0, rows, hs], q_ref[0, :, hs], _NT, preferred_element_type=F32)
                for hs in heads]

    def values_of(rows):
        return [jnp.concatenate([vt_ref[0, hs, rows], ones_rows], axis=0) for hs in heads]

    own_scores = scores_of(block_rows(i))
    own_values = values_of(block_rows(i))
    first_scores = scores_of(block_rows(0))
    gates = [lax.dot_general(km_ref[0, :, hs].astype(BF16), q_ref[0, :, hs], _NT, preferred_element_type=F32)
             for hs in heads]
    init = []
    for hh in range(g):
        picked = _select_topk(jnp.where(past, gates[hh], NEG_INF), n_idx)
        sel = jnp.where(picked & past, 0.0, NEG_INF)
        s = own_scores[hh] - own_ref[hh]
        m0 = jnp.max(s, axis=0, keepdims=True)
        pv = jnp.dot(own_values[hh], jnp.exp2(s - m0).astype(BF16), preferred_element_type=F32)
        u = first_scores[hh] - sd_ref[hh]
        init.append((sel, m0, pv[dh:dh + 1], pv[:dh], u, jnp.max(u, axis=0, keepdims=True)))
    for hh in range(g):
        sel_s[hh], m_s[hh], l_s[hh], acc_s[hh], sc_s[hh], mx_s[hh] = init[hh]

    def past_block(n, carry):
        base = ((i - n) * blk).astype(F32)
        values = values_of(block_rows(n))
        next_scores = scores_of(block_rows(jnp.minimum(n + 1, i - 1)))
        new = []
        for hh in range(g):
            qbias = sel_s[hh, pl.ds(n, 1), :] - slopes[hh] * base
            m_old = m_s[hh]
            m_new = jnp.maximum(m_old, mx_s[hh] + qbias)
            alpha = jnp.exp2(m_old - m_new)
            p = jnp.exp2(sc_s[hh] - (m_new - qbias))
            pv = jnp.dot(values[hh], p.astype(BF16), preferred_element_type=F32)
            u = next_scores[hh] - sd_ref[hh]
            new.append((m_new, alpha * l_s[hh] + pv[dh:dh + 1], alpha * acc_s[hh] + pv[:dh], u,
                        jnp.max(u, axis=0, keepdims=True)))
        for hh in range(g):
            m_s[hh], l_s[hh], acc_s[hh], sc_s[hh], mx_s[hh] = new[hh]
        return carry

    lax.fori_loop(0, i, past_block, 0)

    for hh in range(g):
        o_ref[0, :, hh * dh:(hh + 1) * dh] = (acc_s[hh] / l_s[hh]).T.astype(BF16)


def _alibi_tables(n_heads):
    slopes = 2.0 ** (-8.0 * (np.arange(n_heads) + 1) / n_heads) * LOG2_E
    pos = np.arange(MOBA_BLOCK)
    dist = (pos[None, :] - pos[:, None]).astype(np.float64)
    sd = slopes[:, None, None] * dist
    own = np.where(dist >= 0, sd, -NEG_INF)
    return jnp.asarray(slopes, F32), jnp.asarray(sd, F32), jnp.asarray(own, F32)


def _attention(q, k, vt, km):
    b, t, d = q.shape
    g, blk = HEADS_PER_STEP, MOBA_BLOCK
    nb = t // blk
    gw = g * HEAD_DIM
    assert t % blk == 0 and d % gw == 0
    slopes, sd, own = _alibi_tables(d // HEAD_DIM)
    qo = pl.BlockSpec((1, blk, gw), lambda bi, hi, ti: (bi, ti, hi))
    table = pl.BlockSpec((g, blk, blk), lambda bi, hi, ti: (hi, 0, 0))
    stat = pltpu.VMEM((g, 1, blk), F32)
    return pl.pallas_call(
        functools.partial(_attn_kernel, g=g),
        grid=(b, d // gw, nb),
        in_specs=[pl.BlockSpec(memory_space=pltpu.SMEM), qo,
                  pl.BlockSpec((1, t, gw), lambda bi, hi, ti: (bi, 0, hi)),
                  pl.BlockSpec((1, gw, t), lambda bi, hi, ti: (bi, hi, 0)),
                  pl.BlockSpec((1, nb, gw), lambda bi, hi, ti: (bi, 0, hi)), table, table],
        out_specs=qo,
        out_shape=jax.ShapeDtypeStruct((b, t, d), BF16),
        scratch_shapes=[stat, stat, stat, pltpu.VMEM((g, HEAD_DIM, blk), F32), pltpu.VMEM((g, nb, blk), F32),
                        pltpu.VMEM((g, blk, blk), F32)],
        compiler_params=_params(("parallel", "parallel", "arbitrary")),
        name="moba_attention",
    )(slopes, q, k, vt, km, sd, own)


def _oproj_kernel(x_ref, a_ref, wo_ref, gpost_ref, o_ref):
    y = jnp.dot(a_ref[...], wo_ref[...], preferred_element_type=F32)
    o_ref[...] = x_ref[...] + _rms(y, gpost_ref[...])


def _oproj(x2d, a2d, wo, g_post):
    n, d = x2d.shape
    row = pl.BlockSpec((ROW_TILE, d), lambda i: (i, 0))
    return pl.pallas_call(
        _oproj_kernel,
        grid=(n // ROW_TILE,),
        in_specs=[row, row, _resident((d, d)), _resident((1, d))],
        out_specs=row,
        out_shape=jax.ShapeDtypeStruct((n, d), F32),
        compiler_params=_params(("parallel",)),
        name="attn_out_proj",
    )(x2d, a2d, wo, g_post.reshape(1, d))


def kernel(x, ffn_norm_pre, ffn_norm_post, ffn_w_gate_up, ffn_w_down, mix_norm_pre, mix_norm_post, conv_w_in,
           conv_b_in, conv_w_dw, conv_b_dw, conv_ln_g, conv_ln_b, conv_w_out, conv_b_out, kv_norm, w_kv,
           attn_w_q, attn_w_o):
    b, t, d = x.shape
    depth = ffn_w_gate_up.shape[0]
    n_conv = conv_w_in.shape[0]
    assert depth - n_conv == attn_w_q.shape[0] == 1, "one MoBA layer reading the shared K/V is supported"
    bf = lambda w: w.astype(BF16)
    wgu_all, wd_all = bf(ffn_w_gate_up), bf(ffn_w_down)

    def ffn(x, layer, half):
        return _ffn(x.reshape(b * t, d), ffn_norm_pre, wgu_all, wd_all, ffn_norm_post, layer, half).reshape(b, t, d)

    for layer in range(depth):
        x = ffn(x, layer, 0)
        if layer < n_conv:
            a = layer
            x = _conv_mixer(x, mix_norm_pre[layer], bf(conv_w_in[a]), conv_b_in[a], conv_w_dw[a], conv_b_dw[a],
                            conv_ln_g[a], conv_ln_b[a], bf(conv_w_out[a]), conv_b_out[a], mix_norm_post[layer])
        else:
            j = layer - n_conv
            q, k, vt, km = _qkv(x, mix_norm_pre[layer], kv_norm, bf(attn_w_q[j]), bf(w_kv[:, :d]),
                                bf(w_kv[:, d:].T))
            att = _attention(q, k, vt, km)
            x = _oproj(x.reshape(b * t, d), att.reshape(b * t, d), bf(attn_w_o[j]),
                       mix_norm_post[layer]).reshape(b, t, d)
        x = ffn(x, layer, 1)
    return x
```

```python
import functools

import numpy as np
import jax
import jax.numpy as jnp
from jax import lax
from jax.experimental import pallas as pl
from jax.experimental.pallas import tpu as pltpu

F32 = jnp.float32
BF16 = jnp.bfloat16

RMS_EPS = 1e-6
LN_EPS = 1e-5
MACARON_WEIGHT = 0.5
NEG_INF = -1e30
LOG2_E = 1.4426950408889634
HEAD_DIM = 128
MOBA_BLOCK = 256
MOBA_TOPK = 3

V7X_VMEM_BYTES = 64 * 1024 * 1024
VMEM_LIMIT_BYTES = 56 * 1024 * 1024
LANES = 128
SUBLANES = 8
MXU_COLS = 256
BF16_SUBLANES = 16
CONV_HALO = 32

ROW_TILE = 512
FFN_ROW_TILE = 1024
FF_CHUNK = 256
HEADS_PER_STEP = 8


_NT = (((1,), (1,)), ((), ()))


def _rms(x, g):
    return x * lax.rsqrt(jnp.mean(x * x, axis=-1, keepdims=True) + RMS_EPS) * g


def _resident(shape):
    return pl.BlockSpec(shape, lambda *_: (0,) * len(shape), pipeline_mode=pl.Buffered(1))


def _params(semantics):
    return pltpu.CompilerParams(dimension_semantics=semantics, vmem_limit_bytes=VMEM_LIMIT_BYTES)


def _ffn_kernel(*refs, d_ff, fc, sub, with_proj):
    if with_proj:
        x_ref, att_ref, wo_ref, gmix_ref, gpre_ref, wgu_ref, wd_ref, gpost_ref, o_ref = refs
    else:
        x_ref, gpre_ref, wgu_ref, wd_ref, gpost_ref, o_ref = refs
    for r0 in range(0, x_ref.shape[0], sub):
        x = x_ref[r0:r0 + sub, :]
        if with_proj:
            y = jnp.dot(att_ref[r0:r0 + sub, :], wo_ref[...], preferred_element_type=F32)
            x = x + _rms(y, gmix_ref[...])
        xn = _rms(x, gpre_ref[...]).astype(BF16)
        acc = jnp.zeros(x.shape, F32)
        for f in range(d_ff // fc):
            gate = jnp.dot(xn, wgu_ref[:, f * fc:(f + 1) * fc], preferred_element_type=F32)
            up = jnp.dot(xn, wgu_ref[:, d_ff + f * fc:d_ff + (f + 1) * fc], preferred_element_type=F32)
            h = (gate * jax.nn.sigmoid(gate) * up).astype(BF16)
            acc = acc + jnp.dot(h, wd_ref[f * fc:(f + 1) * fc, :], preferred_element_type=F32)
        o_ref[r0:r0 + sub, :] = x + MACARON_WEIGHT * _rms(acc, gpost_ref[...])


def _ffn(x2d, g_pre, wgu_all, wd_all, g_post, layer, half, proj=None):
    n, d = x2d.shape
    d_ff = wd_all.shape[2]
    tm = FFN_ROW_TILE
    assert n % tm == 0 and tm % ROW_TILE == 0 and d_ff % FF_CHUNK == 0
    row = pl.BlockSpec((tm, d), lambda i: (i, 0))
    pick = lambda i: (layer, half, 0, 0)
    proj_specs, proj_args = [], []
    if proj is not None:
        att2d, wo, g_mix = proj
        proj_specs = [row, _resident((d, d)), _resident((1, d))]
        proj_args = [att2d, wo, g_mix.reshape(1, d)]
    return pl.pallas_call(
        functools.partial(_ffn_kernel, d_ff=d_ff, fc=FF_CHUNK, sub=ROW_TILE, with_proj=proj is not None),
        grid=(n // tm,),
        in_specs=[row] + proj_specs + [
            _resident((1, d)),
            pl.BlockSpec((None, None, d, 2 * d_ff), pick, pipeline_mode=pl.Buffered(1)),
            pl.BlockSpec((None, None, d_ff, d), pick, pipeline_mode=pl.Buffered(1)), _resident((1, d))],
        out_specs=row,
        out_shape=jax.ShapeDtypeStruct((n, d), F32),
        compiler_params=_params(("parallel",)),
        name="swiglu_ffn",
    )(x2d, *proj_args, g_pre[layer, half].reshape(1, d), wgu_all, wd_all, g_post[layer, half].reshape(1, d))


def _conv_kernel(x_ref, gpre_ref, win_ref, bin_ref, wdw_ref, bdw_ref, lng_ref, lnb_ref, wout_ref, bout_ref,
                 gpost_ref, o_ref, ubuf, cbuf, *, tm, kw):
    d = x_ref.shape[-1]
    row_chunk = 64

    @pl.when(pl.program_id(1) == 0)
    def _():
        ubuf[0:CONV_HALO, :] = jnp.zeros((CONV_HALO, d), F32)

    x = x_ref[0]
    h = _rms(x, gpre_ref[...]).astype(BF16)

    first = CONV_HALO - (kw - 1)
    taps_by_shift = {}
    for j in range(kw):
        shift = (first + j) % SUBLANES
        taps_by_shift.setdefault(shift, []).append((j, first + j - shift))

    for c0 in range(0, d, MXU_COLS):
        cols = slice(c0, c0 + MXU_COLS)
        gcols = slice(d + c0, d + c0 + MXU_COLS)
        a = jnp.dot(h, win_ref[:, cols], preferred_element_type=F32) + bin_ref[:, cols]
        gt = jnp.dot(h, win_ref[:, gcols], preferred_element_type=F32) + bin_ref[:, gcols]
        ubuf[CONV_HALO:CONV_HALO + tm, cols] = a * jax.nn.sigmoid(gt)
        for c in range(c0, c0 + MXU_COLS, LANES):
            cs = slice(c, c + LANES)
            for r0 in range(0, tm, row_chunk):
                win = ubuf[r0:r0 + row_chunk + CONV_HALO, cs]
                acc = jnp.broadcast_to(bdw_ref[:, cs], (row_chunk, LANES))
                for shift, taps in sorted(taps_by_shift.items()):
                    rows = row_chunk + (SUBLANES if shift else 0)
                    part = None
                    for j, start in taps:
                        term = wdw_ref[j:j + 1, cs] * win[start:start + rows]
                        part = term if part is None else part + term
                    acc = acc + part[shift:shift + row_chunk]
                cbuf[r0:r0 + row_chunk, cs] = acc
        ubuf[0:CONV_HALO, cols] = ubuf[tm:tm + CONV_HALO, cols]

    cv = cbuf[...]
    mu = jnp.mean(cv, axis=-1, keepdims=True)
    cen = cv - mu
    var = jnp.mean(cen * cen, axis=-1, keepdims=True)
    un = cen * lax.rsqrt(var + LN_EPS) * lng_ref[...] + lnb_ref[...]
    s = (un * jax.nn.sigmoid(un)).astype(BF16)
    y = jnp.dot(s, wout_ref[...], preferred_element_type=F32) + bout_ref[...]
    o_ref[0] = x + _rms(y, gpost_ref[...])


def _conv_mixer(x, g_pre, w_in, b_in, w_dw, b_dw, ln_g, ln_b, w_out, b_out, g_post):
    b, t, d = x.shape
    kw = w_dw.shape[0]
    tm = ROW_TILE
    assert t % tm == 0 and kw - 1 <= CONV_HALO <= tm and d % LANES == 0
    row = pl.BlockSpec((1, tm, d), lambda bi, ti: (bi, ti, 0))
    vec = lambda a: a.reshape(1, -1)
    return pl.pallas_call(
        functools.partial(_conv_kernel, tm=tm, kw=kw),
        grid=(b, t // tm),
        in_specs=[row, _resident((1, d)), _resident((d, 2 * d)), _resident((1, 2 * d)), _resident((kw, d)),
                  _resident((1, d)), _resident((1, d)), _resident((1, d)), _resident((d, d)), _resident((1, d)),
                  _resident((1, d))],
        out_specs=row,
        out_shape=jax.ShapeDtypeStruct((b, t, d), F32),
        scratch_shapes=[pltpu.VMEM((CONV_HALO + tm, d), F32), pltpu.VMEM((tm, d), F32)],
        compiler_params=_params(("arbitrary", "arbitrary")),
        name="conformer_conv",
    )(x, vec(g_pre), w_in, vec(b_in), w_dw, vec(b_dw), vec(ln_g), vec(ln_b), w_out, vec(b_out), vec(g_post))


def _qkv_kernel(x_ref, gmix_ref, gkv_ref, wq_ref, wk_ref, wvt_ref, q_ref, k_ref, vt_ref, km_ref, *, scale):
    x = x_ref[0]
    tm, d = x.shape
    r = x * lax.rsqrt(jnp.mean(x * x, axis=-1, keepdims=True) + RMS_EPS)
    hq = (r * gmix_ref[...]).astype(BF16)
    hkv = (r * gkv_ref[...]).astype(BF16)
    q_ref[0] = (jnp.dot(hq, wq_ref[...], preferred_element_type=F32) * scale).astype(BF16)
    k = jnp.dot(hkv, wk_ref[...], preferred_element_type=F32)
    k_ref[0] = k.astype(BF16)
    km_ref[0, 0] = jnp.mean(k.reshape(tm // MOBA_BLOCK, MOBA_BLOCK, d), axis=1)
    vt_ref[0] = lax.dot_general(wvt_ref[...], hkv, _NT, preferred_element_type=F32).astype(BF16)


def _qkv(x, g_mix, g_kv, wq, wk, wvt):
    b, t, d = x.shape
    tm = ROW_TILE
    assert t % tm == 0 and tm % MOBA_BLOCK == 0
    per = tm // MOBA_BLOCK
    row = pl.BlockSpec((1, tm, d), lambda bi, ti: (bi, ti, 0))
    q, k, vt, km = pl.pallas_call(
        functools.partial(_qkv_kernel, scale=HEAD_DIM ** -0.5 * LOG2_E),
        grid=(b, t // tm),
        in_specs=[row, _resident((1, d)), _resident((1, d)), _resident((d, d)), _resident((d, d)),
                  _resident((d, d))],
        out_specs=[row, row, pl.BlockSpec((1, d, tm), lambda bi, ti: (bi, 0, ti)),
                   pl.BlockSpec((1, 1, per, d), lambda bi, ti: (bi, ti, 0, 0))],
        out_shape=[jax.ShapeDtypeStruct((b, t, d), BF16), jax.ShapeDtypeStruct((b, t, d), BF16),
                   jax.ShapeDtypeStruct((b, d, t), BF16), jax.ShapeDtypeStruct((b, t // tm, per, d), F32)],
        compiler_params=_params(("parallel", "parallel")),
        name="qkv_proj",
    )(x, g_mix.reshape(1, d), g_kv.reshape(1, d), wq, wk, wvt)
    return q, k, vt, km.reshape(b, t // MOBA_BLOCK, d)


def _select_topk(gate, n_idx):
    picked = jnp.zeros(gate.shape, jnp.bool_)
    for _ in range(MOBA_TOPK):
        best = jnp.max(gate, axis=0, keepdims=True)
        first = jnp.min(jnp.where(gate == best, n_idx, gate.shape[0]), axis=0, keepdims=True)
        hit = n_idx == first
        picked = picked | hit
        gate = jnp.where(hit, -jnp.inf, gate)
    return picked


def _attn_kernel(slopes_ref, q_ref, k_ref, vt_ref, km_ref, sd_ref, own_ref, o_ref, m_s, l_s, mx_s, acc_s, sel_s,
                 sc_s, *, g):
    blk, dh = MOBA_BLOCK, HEAD_DIM
    hg = pl.program_id(1)
    i = pl.program_id(2)
    nb = km_ref.shape[1]
    n_idx = lax.broadcasted_iota(jnp.int32, (nb, blk), 0)
    past = n_idx < i
    ones_rows = jnp.ones((BF16_SUBLANES, blk), BF16)

    heads = [slice(hh * dh, (hh + 1) * dh) for hh in range(g)]
    slopes = [slopes_ref[hg * g + hh] for hh in range(g)]

    def block_rows(n):
        return pl.ds(pl.multiple_of(n * blk, blk), blk)

    def scores_of(rows):
        return [lax.dot_general(k_ref[0, rows, hs], q_ref[0, :, hs], _NT, preferred_element_type=F32)
                for hs in heads]

    def values_of(rows):
        return [jnp.concatenate([vt_ref[0, hs, rows], ones_rows], axis=0) for hs in heads]

    own_scores = scores_of(block_rows(i))
    own_values = values_of(block_rows(i))
    first_scores = scores_of(block_rows(0))
    gates = [lax.dot_general(km_ref[0, :, hs].astype(BF16), q_ref[0, :, hs], _NT, preferred_element_type=F32)
             for hs in heads]
    init = []
    for hh in range(g):
        picked = _select_topk(jnp.where(past, gates[hh], NEG_INF), n_idx)
        sel = jnp.where(picked & past, 0.0, NEG_INF)
        s = own_scores[hh] - own_ref[hh]
        m0 = jnp.max(s, axis=0, keepdims=True)
        pv = jnp.dot(own_values[hh], jnp.exp2(s - m0).astype(BF16), preferred_element_type=F32)
        u = first_scores[hh] - sd_ref[hh]
        init.append((sel, m0, pv[dh:dh + 1], pv[:dh], u, jnp.max(u, axis=0, keepdims=True)))
    for hh in range(g):
        sel_s[hh], m_s[hh], l_s[hh], acc_s[hh], sc_s[hh], mx_s[hh] = init[hh]

    def past_block(n, carry):
        base = ((i - n) * blk).astype(F32)
        values = values_of(block_rows(n))
        next_scores = scores_of(block_rows(jnp.minimum(n + 1, i - 1)))
        new = []
        for hh in range(g):
            qbias = sel_s[hh, pl.ds(n, 1), :] - slopes[hh] * base
            m_old = m_s[hh]
            m_new = jnp.maximum(m_old, mx_s[hh] + qbias)
            alpha = jnp.exp2(m_old - m_new)
            p = jnp.exp2(sc_s[hh] - (m_new - qbias))
            pv = jnp.dot(values[hh], p.astype(BF16), preferred_element_type=F32)
            u = next_scores[hh] - sd_ref[hh]
            new.append((m_new, alpha * l_s[hh] + pv[dh:dh + 1], alpha * acc_s[hh] + pv[:dh], u,
                        jnp.max(u, axis=0, keepdims=True)))
        for hh in range(g):
            m_s[hh], l_s[hh], acc_s[hh], sc_s[hh], mx_s[hh] = new[hh]
        return carry

    lax.fori_loop(0, i, past_block, 0)

    for hh in range(g):
        o_ref[0, :, hh * dh:(hh + 1) * dh] = (acc_s[hh] / l_s[hh]).T.astype(BF16)


def _alibi_tables(n_heads):
    slopes = 2.0 ** (-8.0 * (np.arange(n_heads) + 1) / n_heads) * LOG2_E
    pos = np.arange(MOBA_BLOCK)
    dist = (pos[None, :] - pos[:, None]).astype(np.float64)
    sd = slopes[:, None, None] * dist
    own = np.where(dist >= 0, sd, -NEG_INF)
    return jnp.asarray(slopes, F32), jnp.asarray(sd, F32), jnp.asarray(own, F32)


def _attention(q, k, vt, km):
    b, t, d = q.shape
    g, blk = HEADS_PER_STEP, MOBA_BLOCK
    nb = t // blk
    gw = g * HEAD_DIM
    assert t % blk == 0 and d % gw == 0
    slopes, sd, own = _alibi_tables(d // HEAD_DIM)
    qo = pl.BlockSpec((1, blk, gw), lambda bi, hi, ti: (bi, ti, hi))
    table = pl.BlockSpec((g, blk, blk), lambda bi, hi, ti: (hi, 0, 0))
    stat = pltpu.VMEM((g, 1, blk), F32)
    return pl.pallas_call(
        functools.partial(_attn_kernel, g=g),
        grid=(b, d // gw, nb),
        in_specs=[pl.BlockSpec(memory_space=pltpu.SMEM), qo,
                  pl.BlockSpec((1, t, gw), lambda bi, hi, ti: (bi, 0, hi)),
                  pl.BlockSpec((1, gw, t), lambda bi, hi, ti: (bi, hi, 0)),
                  pl.BlockSpec((1, nb, gw), lambda bi, hi, ti: (bi, 0, hi)), table, table],
        out_specs=qo,
        out_shape=jax.ShapeDtypeStruct((b, t, d), BF16),
        scratch_shapes=[stat, stat, stat, pltpu.VMEM((g, HEAD_DIM, blk), F32), pltpu.VMEM((g, nb, blk), F32),
                        pltpu.VMEM((g, blk, blk), F32)],
        compiler_params=_params(("parallel", "parallel", "arbitrary")),
        name="moba_attention",
    )(slopes, q, k, vt, km, sd, own)


def kernel(x, ffn_norm_pre, ffn_norm_post, ffn_w_gate_up, ffn_w_down, mix_norm_pre, mix_norm_post, conv_w_in,
           conv_b_in, conv_w_dw, conv_b_dw, conv_ln_g, conv_ln_b, conv_w_out, conv_b_out, kv_norm, w_kv,
           attn_w_q, attn_w_o):
    b, t, d = x.shape
    depth = ffn_w_gate_up.shape[0]
    n_conv = conv_w_in.shape[0]
    assert depth - n_conv == attn_w_q.shape[0] == 1, "one MoBA layer reading the shared K/V is supported"
    bf = lambda w: w.astype(BF16)
    wgu_all, wd_all = bf(ffn_w_gate_up), bf(ffn_w_down)

    def ffn(x, layer, half, proj=None):
        return _ffn(x.reshape(b * t, d), ffn_norm_pre, wgu_all, wd_all, ffn_norm_post, layer, half,
                    proj).reshape(b, t, d)

    for layer in range(depth):
        x = ffn(x, layer, 0)
        if layer < n_conv:
            a = layer
            x = _conv_mixer(x, mix_norm_pre[layer], bf(conv_w_in[a]), conv_b_in[a], conv_w_dw[a], conv_b_dw[a],
                            conv_ln_g[a], conv_ln_b[a], bf(conv_w_out[a]), conv_b_out[a], mix_norm_post[layer])
            x = ffn(x, layer, 1)
        else:
            j = layer - n_conv
            q, k, vt, km = _qkv(x, mix_norm_pre[layer], kv_norm, bf(attn_w_q[j]), bf(w_kv[:, :d]),
                                bf(w_kv[:, d:].T))
            att = _attention(q, k, vt, km)
            x = ffn(x, layer, 1, (att.reshape(b * t, d), bf(attn_w_o[j]), mix_norm_post[layer]))
    return x
```

```python
import functools

import numpy as np
import jax
import jax.numpy as jnp
from jax import lax
from jax.experimental import pallas as pl
from jax.experimental.pallas import tpu as pltpu

F32 = jnp.float32
BF16 = jnp.bfloat16

RMS_EPS = 1e-6
LN_EPS = 1e-5
MACARON_WEIGHT = 0.5
NEG_INF = -1e30
LOG2_E = 1.4426950408889634
HEAD_DIM = 128
MOBA_BLOCK = 256
MOBA_TOPK = 3

V7X_VMEM_BYTES = 64 * 1024 * 1024
VMEM_LIMIT_BYTES = 56 * 1024 * 1024
LANES = 128
SUBLANES = 8
MXU_COLS = 256
BF16_SUBLANES = 16
CONV_HALO = 32

ROW_TILE = 512
FFN_ROW_TILE = 1024
FF_CHUNK = 256
HEADS_PER_STEP = 8


_NT = (((1,), (1,)), ((), ()))


def _rms(x, g):
    return x * lax.rsqrt(jnp.mean(x * x, axis=-1, keepdims=True) + RMS_EPS) * g


def _resident(shape):
    return pl.BlockSpec(shape, lambda *_: (0,) * len(shape), pipeline_mode=pl.Buffered(1))


def _params(semantics):
    return pltpu.CompilerParams(dimension_semantics=semantics, vmem_limit_bytes=VMEM_LIMIT_BYTES)


def _ffn_kernel(*refs, d_ff, fc, sub, with_proj, with_cast):
    refs = list(refs)
    x_ref = refs.pop(0)
    att_ref, wo_ref, gmix_ref = [refs.pop(0) for _ in range(3)] if with_proj else (None, None, None)
    gpre_ref, wgu_ref, wd_ref, gpost_ref = [refs.pop(0) for _ in range(4)]
    cast_in = [refs.pop(0) for _ in range(2)] if with_cast else []
    o_ref = refs.pop(0)
    for src_ref, dst_ref in zip(cast_in, refs):
        dst_ref[...] = src_ref[...].astype(BF16)
    tiles = []
    for r0 in range(0, x_ref.shape[0], sub):
        x = x_ref[r0:r0 + sub, :]
        if with_proj:
            y = jnp.dot(att_ref[r0:r0 + sub, :], wo_ref[...], preferred_element_type=F32)
            x = x + _rms(y, gmix_ref[...])
        tiles.append((r0, x, _rms(x, gpre_ref[...]).astype(BF16)))
    for r0, x, xn in tiles:
        acc = jnp.zeros(x.shape, F32)
        for f in range(d_ff // fc):
            gate = jnp.dot(xn, wgu_ref[:, f * fc:(f + 1) * fc], preferred_element_type=F32)
            up = jnp.dot(xn, wgu_ref[:, d_ff + f * fc:d_ff + (f + 1) * fc], preferred_element_type=F32)
            h = (gate * jax.nn.sigmoid(gate) * up).astype(BF16)
            acc = acc + jnp.dot(h, wd_ref[f * fc:(f + 1) * fc, :], preferred_element_type=F32)
        o_ref[r0:r0 + sub, :] = x + MACARON_WEIGHT * _rms(acc, gpost_ref[...])


def _ffn(x2d, g_pre, wgu, wd, g_post, proj=None, cast_next=None):
    n, d = x2d.shape
    d_ff = wd.shape[0]
    tm = FFN_ROW_TILE
    steps = n // tm
    assert n % tm == 0 and tm % ROW_TILE == 0 and d_ff % FF_CHUNK == 0
    row = pl.BlockSpec((tm, d), lambda i: (i, 0))
    proj_specs, proj_args = [], []
    if proj is not None:
        att2d, wo, g_mix = proj
        proj_specs = [row, _resident((d, d)), _resident((1, d))]
        proj_args = [att2d, wo, g_mix.reshape(1, d)]
    cast_specs, cast_args, cast_out_specs, cast_out_shapes = [], [], [], []
    if cast_next is not None:
        wgu_all, wd_all, layer, half = cast_next
        for w_all in (wgu_all, wd_all):
            rows, cols = w_all.shape[2:]
            assert rows % (steps * BF16_SUBLANES) == 0
            cast_specs.append(pl.BlockSpec((None, None, rows // steps, cols), lambda i: (layer, half, i, 0)))
            cast_args.append(w_all)
            cast_out_specs.append(pl.BlockSpec((rows // steps, cols), lambda i: (i, 0)))
            cast_out_shapes.append(jax.ShapeDtypeStruct((rows, cols), BF16))
    outs = pl.pallas_call(
        functools.partial(_ffn_kernel, d_ff=d_ff, fc=FF_CHUNK, sub=ROW_TILE, with_proj=proj is not None,
                          with_cast=cast_next is not None),
        grid=(steps,),
        in_specs=[row] + proj_specs + [_resident((1, d)), _resident((d, 2 * d_ff)), _resident((d_ff, d)),
                                       _resident((1, d))] + cast_specs,
        out_specs=[row] + cast_out_specs,
        out_shape=[jax.ShapeDtypeStruct((n, d), F32)] + cast_out_shapes,
        compiler_params=_params(("parallel",)),
        name="swiglu_ffn",
    )(x2d, *proj_args, g_pre.reshape(1, d), wgu, wd, g_post.reshape(1, d), *cast_args)
    return outs[0], tuple(outs[1:])


def _conv_kernel(x_ref, gpre_ref, win_ref, bin_ref, wdw_ref, bdw_ref, lng_ref, lnb_ref, wout_ref, bout_ref,
                 gpost_ref, o_ref, ubuf, cbuf, *, tm, kw):
    d = x_ref.shape[-1]
    row_chunk = 64

    @pl.when(pl.program_id(1) == 0)
    def _():
        ubuf[0:CONV_HALO, :] = jnp.zeros((CONV_HALO, d), F32)

    x = x_ref[0]
    h = _rms(x, gpre_ref[...]).astype(BF16)

    first = CONV_HALO - (kw - 1)
    taps_by_shift = {}
    for j in range(kw):
        shift = (first + j) % SUBLANES
        taps_by_shift.setdefault(shift, []).append((j, first + j - shift))

    for c0 in range(0, d, MXU_COLS):
        cols = slice(c0, c0 + MXU_COLS)
        gcols = slice(d + c0, d + c0 + MXU_COLS)
        a = jnp.dot(h, win_ref[:, cols], preferred_element_type=F32) + bin_ref[:, cols]
        gt = jnp.dot(h, win_ref[:, gcols], preferred_element_type=F32) + bin_ref[:, gcols]
        ubuf[CONV_HALO:CONV_HALO + tm, cols] = a * jax.nn.sigmoid(gt)
        for c in range(c0, c0 + MXU_COLS, LANES):
            cs = slice(c, c + LANES)
            for r0 in range(0, tm, row_chunk):
                win = ubuf[r0:r0 + row_chunk + CONV_HALO, cs]
                acc = jnp.broadcast_to(bdw_ref[:, cs], (row_chunk, LANES))
                for shift, taps in sorted(taps_by_shift.items()):
                    rows = row_chunk + (SUBLANES if shift else 0)
                    part = None
                    for j, start in taps:
                        term = wdw_ref[j:j + 1, cs] * win[start:start + rows]
                        part = term if part is None else part + term
                    acc = acc + part[shift:shift + row_chunk]
                cbuf[r0:r0 + row_chunk, cs] = acc
        ubuf[0:CONV_HALO, cols] = ubuf[tm:tm + CONV_HALO, cols]

    cv = cbuf[...]
    mu = jnp.mean(cv, axis=-1, keepdims=True)
    cen = cv - mu
    var = jnp.mean(cen * cen, axis=-1, keepdims=True)
    un = cen * lax.rsqrt(var + LN_EPS) * lng_ref[...] + lnb_ref[...]
    s = (un * jax.nn.sigmoid(un)).astype(BF16)
    y = jnp.dot(s, wout_ref[...], preferred_element_type=F32) + bout_ref[...]
    o_ref[0] = x + _rms(y, gpost_ref[...])


def _conv_mixer(x, g_pre, w_in, b_in, w_dw, b_dw, ln_g, ln_b, w_out, b_out, g_post):
    b, t, d = x.shape
    kw = w_dw.shape[0]
    tm = ROW_TILE
    assert t % tm == 0 and kw - 1 <= CONV_HALO <= tm and d % LANES == 0
    row = pl.BlockSpec((1, tm, d), lambda bi, ti: (bi, ti, 0))
    vec = lambda a: a.reshape(1, -1)
    return pl.pallas_call(
        functools.partial(_conv_kernel, tm=tm, kw=kw),
        grid=(b, t // tm),
        in_specs=[row, _resident((1, d)), _resident((d, 2 * d)), _resident((1, 2 * d)), _resident((kw, d)),
                  _resident((1, d)), _resident((1, d)), _resident((1, d)), _resident((d, d)), _resident((1, d)),
                  _resident((1, d))],
        out_specs=row,
        out_shape=jax.ShapeDtypeStruct((b, t, d), F32),
        scratch_shapes=[pltpu.VMEM((CONV_HALO + tm, d), F32), pltpu.VMEM((tm, d), F32)],
        compiler_params=_params(("arbitrary", "arbitrary")),
        name="conformer_conv",
    )(x, vec(g_pre), w_in, vec(b_in), w_dw, vec(b_dw), vec(ln_g), vec(ln_b), w_out, vec(b_out), vec(g_post))


def _qkv_kernel(x_ref, gmix_ref, gkv_ref, wq_ref, wk_ref, wvt_ref, q_ref, k_ref, vt_ref, km_ref, *, scale):
    x = x_ref[0]
    tm, d = x.shape
    r = x * lax.rsqrt(jnp.mean(x * x, axis=-1, keepdims=True) + RMS_EPS)
    hq = (r * gmix_ref[...]).astype(BF16)
    hkv = (r * gkv_ref[...]).astype(BF16)
    q_ref[0] = (jnp.dot(hq, wq_ref[...], preferred_element_type=F32) * scale).astype(BF16)
    k = jnp.dot(hkv, wk_ref[...], preferred_element_type=F32)
    k_ref[0] = k.astype(BF16)
    km_ref[0, 0] = jnp.mean(k.reshape(tm // MOBA_BLOCK, MOBA_BLOCK, d), axis=1)
    vt_ref[0] = lax.dot_general(wvt_ref[...], hkv, _NT, preferred_element_type=F32).astype(BF16)


def _qkv(x, g_mix, g_kv, wq, wk, wvt):
    b, t, d = x.shape
    tm = ROW_TILE
    assert t % tm == 0 and tm % MOBA_BLOCK == 0
    per = tm // MOBA_BLOCK
    row = pl.BlockSpec((1, tm, d), lambda bi, ti: (bi, ti, 0))
    q, k, vt, km = pl.pallas_call(
        functools.partial(_qkv_kernel, scale=HEAD_DIM ** -0.5 * LOG2_E),
        grid=(b, t // tm),
        in_specs=[row, _resident((1, d)), _resident((1, d)), _resident((d, d)), _resident((d, d)),
                  _resident((d, d))],
        out_specs=[row, row, pl.BlockSpec((1, d, tm), lambda bi, ti: (bi, 0, ti)),
                   pl.BlockSpec((1, 1, per, d), lambda bi, ti: (bi, ti, 0, 0))],
        out_shape=[jax.ShapeDtypeStruct((b, t, d), BF16), jax.ShapeDtypeStruct((b, t, d), BF16),
                   jax.ShapeDtypeStruct((b, d, t), BF16), jax.ShapeDtypeStruct((b, t // tm, per, d), F32)],
        compiler_params=_params(("parallel", "parallel")),
        name="qkv_proj",
    )(x, g_mix.reshape(1, d), g_kv.reshape(1, d), wq, wk, wvt)
    return q, k, vt, km.reshape(b, t // MOBA_BLOCK, d)


def _select_topk(gate, n_idx):
    picked = jnp.zeros(gate.shape, jnp.bool_)
    for _ in range(MOBA_TOPK):
        best = jnp.max(gate, axis=0, keepdims=True)
        first = jnp.min(jnp.where(gate == best, n_idx, gate.shape[0]), axis=0, keepdims=True)
        hit = n_idx == first
        picked = picked | hit
        gate = jnp.where(hit, -jnp.inf, gate)
    return picked


def _attn_kernel(slopes_ref, q_ref, k_ref, vt_ref, km_ref, sd_ref, own_ref, o_ref, m_s, l_s, mx_s, acc_s, sel_s,
                 sc_s, *, g):
    blk, dh = MOBA_BLOCK, HEAD_DIM
    hg = pl.program_id(1)
    i = pl.program_id(2)
    nb = km_ref.shape[1]
    n_idx = lax.broadcasted_iota(jnp.int32, (nb, blk), 0)
    past = n_idx < i
    ones_rows = jnp.ones((BF16_SUBLANES, blk), BF16)

    heads = [slice(hh * dh, (hh + 1) * dh) for hh in range(g)]
    slopes = [slopes_ref[hg * g + hh] for hh in range(g)]

    def block_rows(n):
        return pl.ds(pl.multiple_of(n * blk, blk), blk)

    def scores_of(rows):
        return [lax.dot_general(k_ref[0, rows, hs], q_ref[0, :, hs], _NT, preferred_element_type=F32)
                for hs in heads]

    def values_of(rows):
        return [jnp.concatenate([vt_ref[0, hs, rows], ones_rows], axis=0) for hs in heads]

    own_scores = scores_of(block_rows(i))
    own_values = values_of(block_rows(i))
    first_scores = scores_of(block_rows(0))
    gates = [lax.dot_general(km_ref[0, :, hs].astype(BF16), q_ref[0, :, hs], _NT, preferred_element_type=F32)
             for hs in heads]
    init = []
    for hh in range(g):
        picked = _select_topk(jnp.where(past, gates[hh], NEG_INF), n_idx)
        sel = jnp.where(picked & past, 0.0, NEG_INF)
        s = own_scores[hh] - own_ref[hh]
        m0 = jnp.max(s, axis=0, keepdims=True)
        pv = jnp.dot(own_values[hh], jnp.exp2(s - m0).astype(BF16), preferred_element_type=F32)
        u = first_scores[hh] - sd_ref[hh]
        init.append((sel, m0, pv[dh:dh + 1], pv[:dh], u, jnp.max(u, axis=0, keepdims=True)))
    for hh in range(g):
        sel_s[hh], m_s[hh], l_s[hh], acc_s[hh], sc_s[hh], mx_s[hh] = init[hh]

    def past_block(n, carry):
        base = ((i - n) * blk).astype(F32)
        values = values_of(block_rows(n))
        next_scores = scores_of(block_rows(jnp.minimum(n + 1, i - 1)))
        new = []
        for hh in range(g):
            qbias = sel_s[hh, pl.ds(n, 1), :] - slopes[hh] * base
            m_old = m_s[hh]
            m_new = jnp.maximum(m_old, mx_s[hh] + qbias)
            alpha = jnp.exp2(m_old - m_new)
            p = jnp.exp2(sc_s[hh] - (m_new - qbias))
            pv = jnp.dot(values[hh], p.astype(BF16), preferred_element_type=F32)
            u = next_scores[hh] - sd_ref[hh]
            new.append((m_new, alpha * l_s[hh] + pv[dh:dh + 1], alpha * acc_s[hh] + pv[:dh], u,
                        jnp.max(u, axis=0, keepdims=True)))
        for hh in range(g):
            m_s[hh], l_s[hh], acc_s[hh], sc_s[hh], mx_s[hh] = new[hh]
        return carry

    lax.fori_loop(0, i, past_block, 0)

    for hh in range(g):
        o_ref[0, :, hh * dh:(hh + 1) * dh] = (acc_s[hh] / l_s[hh]).T.astype(BF16)


def _alibi_tables(n_heads):
    slopes = 2.0 ** (-8.0 * (np.arange(n_heads) + 1) / n_heads) * LOG2_E
    pos = np.arange(MOBA_BLOCK)
    dist = (pos[None, :] - pos[:, None]).astype(np.float64)
    sd = slopes[:, None, None] * dist
    own = np.where(dist >= 0, sd, -NEG_INF)
    return jnp.asarray(slopes, F32), jnp.asarray(sd, F32), jnp.asarray(own, F32)


def _attention(q, k, vt, km):
    b, t, d = q.shape
    g, blk = HEADS_PER_STEP, MOBA_BLOCK
    nb = t // blk
    gw = g * HEAD_DIM
    assert t % blk == 0 and d % gw == 0
    slopes, sd, own = _alibi_tables(d // HEAD_DIM)
    qo = pl.BlockSpec((1, blk, gw), lambda bi, hi, ti: (bi, ti, hi))
    table = pl.BlockSpec((g, blk, blk), lambda bi, hi, ti: (hi, 0, 0))
    stat = pltpu.VMEM((g, 1, blk), F32)
    return pl.pallas_call(
        functools.partial(_attn_kernel, g=g),
        grid=(b, d // gw, nb),
        in_specs=[pl.BlockSpec(memory_space=pltpu.SMEM), qo,
                  pl.BlockSpec((1, t, gw), lambda bi, hi, ti: (bi, 0, hi)),
                  pl.BlockSpec((1, gw, t), lambda bi, hi, ti: (bi, hi, 0)),
                  pl.BlockSpec((1, nb, gw), lambda bi, hi, ti: (bi, 0, hi)), table, table],
        out_specs=qo,
        out_shape=jax.ShapeDtypeStruct((b, t, d), BF16),
        scratch_shapes=[stat, stat, stat, pltpu.VMEM((g, HEAD_DIM, blk), F32), pltpu.VMEM((g, nb, blk), F32),
                        pltpu.VMEM((g, blk, blk), F32)],
        compiler_params=_params(("parallel", "parallel", "arbitrary")),
        name="moba_attention",
    )(slopes, q, k, vt, km, sd, own)


def kernel(x, ffn_norm_pre, ffn_norm_post, ffn_w_gate_up, ffn_w_down, mix_norm_pre, mix_norm_post, conv_w_in,
           conv_b_in, conv_w_dw, conv_b_dw, conv_ln_g, conv_ln_b, conv_w_out, conv_b_out, kv_norm, w_kv,
           attn_w_q, attn_w_o):
    b, t, d = x.shape
    depth = ffn_w_gate_up.shape[0]
    n_conv = conv_w_in.shape[0]
    assert depth - n_conv == attn_w_q.shape[0] == 1, "one MoBA layer reading the shared K/V is supported"
    bf = lambda w: w.astype(BF16)
    halves = [(layer, half) for layer in range(depth) for half in range(2)]
    weights = (bf(ffn_w_gate_up[0, 0]), bf(ffn_w_down[0, 0]))

    def ffn(x, layer, half, proj=None):
        nonlocal weights
        nxt = halves.index((layer, half)) + 1
        cast_next = (ffn_w_gate_up, ffn_w_down) + halves[nxt] if nxt < len(halves) else None
        y, cast = _ffn(x.reshape(b * t, d), ffn_norm_pre[layer, half], *weights, ffn_norm_post[layer, half], proj,
                       cast_next)
        weights = cast
        return y.reshape(b, t, d)

    for layer in range(depth):
        x = ffn(x, layer, 0)
        if layer < n_conv:
            a = layer
            x = _conv_mixer(x, mix_norm_pre[layer], bf(conv_w_in[a]), conv_b_in[a], conv_w_dw[a], conv_b_dw[a],
                            conv_ln_g[a], conv_ln_b[a], bf(conv_w_out[a]), conv_b_out[a], mix_norm_post[layer])
            x = ffn(x, layer, 1)
        else:
            j = layer - n_conv
            q, k, vt, km = _qkv(x, mix_norm_pre[layer], kv_norm, bf(attn_w_q[j]), bf(w_kv[:, :d]),
                                bf(w_kv[:, d:].T))
            att = _attention(q, k, vt, km)
            x = ffn(x, layer, 1, (att.reshape(b * t, d), bf(attn_w_o[j]), mix_norm_post[layer]))
    return x
```

```python
import functools

import numpy as np
import jax
import jax.numpy as jnp
from jax import lax
from jax.experimental import pallas as pl
from jax.experimental.pallas import tpu as pltpu

F32 = jnp.float32
BF16 = jnp.bfloat16

RMS_EPS = 1e-6
LN_EPS = 1e-5
MACARON_WEIGHT = 0.5
NEG_INF = -1e30
LOG2_E = 1.4426950408889634
HEAD_DIM = 128
MOBA_BLOCK = 256
MOBA_TOPK = 3

V7X_VMEM_BYTES = 64 * 1024 * 1024
VMEM_LIMIT_BYTES = 56 * 1024 * 1024
LANES = 128
SUBLANES = 8
MXU_COLS = 256
BF16_SUBLANES = 16
CONV_HALO = 32

ROW_TILE = 512
WIDE_ROW_TILE = 1024
FF_CHUNK = 256
HEADS_PER_STEP = 8


_NT = (((1,), (1,)), ((), ()))


def _rms(x, g):
    return x * lax.rsqrt(jnp.mean(x * x, axis=-1, keepdims=True) + RMS_EPS) * g


def _resident(shape):
    return pl.BlockSpec(shape, lambda *_: (0,) * len(shape), pipeline_mode=pl.Buffered(1))


def _params(semantics):
    return pltpu.CompilerParams(dimension_semantics=semantics, vmem_limit_bytes=VMEM_LIMIT_BYTES)


def _ffn_kernel(*refs, d_ff, fc, sub, with_proj, with_cast):
    refs = list(refs)
    x_ref = refs.pop(0)
    att_ref, wo_ref, gmix_ref = [refs.pop(0) for _ in range(3)] if with_proj else (None, None, None)
    gpre_ref, wgu_ref, wd_ref, gpost_ref = [refs.pop(0) for _ in range(4)]
    cast_in = [refs.pop(0) for _ in range(2)] if with_cast else []
    o_ref = refs.pop(0)
    for src_ref, dst_ref in zip(cast_in, refs):
        dst_ref[...] = src_ref[...].astype(BF16)
    tiles = []
    for r0 in range(0, x_ref.shape[0], sub):
        x = x_ref[r0:r0 + sub, :]
        if with_proj:
            y = jnp.dot(att_ref[r0:r0 + sub, :], wo_ref[...], preferred_element_type=F32)
            x = x + _rms(y, gmix_ref[...])
        tiles.append((r0, x, _rms(x, gpre_ref[...]).astype(BF16)))
    for r0, x, xn in tiles:
        acc = jnp.zeros(x.shape, F32)
        for f in range(d_ff // fc):
            gate = jnp.dot(xn, wgu_ref[:, f * fc:(f + 1) * fc], preferred_element_type=F32)
            up = jnp.dot(xn, wgu_ref[:, d_ff + f * fc:d_ff + (f + 1) * fc], preferred_element_type=F32)
            h = (gate * jax.nn.sigmoid(gate) * up).astype(BF16)
            acc = acc + jnp.dot(h, wd_ref[f * fc:(f + 1) * fc, :], preferred_element_type=F32)
        o_ref[r0:r0 + sub, :] = x + MACARON_WEIGHT * _rms(acc, gpost_ref[...])


def _ffn(x2d, g_pre, wgu, wd, g_post, proj=None, cast_next=None):
    n, d = x2d.shape
    d_ff = wd.shape[0]
    tm = WIDE_ROW_TILE
    steps = n // tm
    assert n % tm == 0 and tm % ROW_TILE == 0 and d_ff % FF_CHUNK == 0
    row = pl.BlockSpec((tm, d), lambda i: (i, 0))
    proj_specs, proj_args = [], []
    if proj is not None:
        att2d, wo, g_mix = proj
        proj_specs = [row, _resident((d, d)), _resident((1, d))]
        proj_args = [att2d, wo, g_mix.reshape(1, d)]
    cast_specs, cast_args, cast_out_specs, cast_out_shapes = [], [], [], []
    if cast_next is not None:
        wgu_all, wd_all, layer, half = cast_next
        for w_all in (wgu_all, wd_all):
            rows, cols = w_all.shape[2:]
            assert rows % (steps * BF16_SUBLANES) == 0
            cast_specs.append(pl.BlockSpec((None, None, rows // steps, cols), lambda i: (layer, half, i, 0)))
            cast_args.append(w_all)
            cast_out_specs.append(pl.BlockSpec((rows // steps, cols), lambda i: (i, 0)))
            cast_out_shapes.append(jax.ShapeDtypeStruct((rows, cols), BF16))
    outs = pl.pallas_call(
        functools.partial(_ffn_kernel, d_ff=d_ff, fc=FF_CHUNK, sub=ROW_TILE, with_proj=proj is not None,
                          with_cast=cast_next is not None),
        grid=(steps,),
        in_specs=[row] + proj_specs + [_resident((1, d)), _resident((d, 2 * d_ff)), _resident((d_ff, d)),
                                       _resident((1, d))] + cast_specs,
        out_specs=[row] + cast_out_specs,
        out_shape=[jax.ShapeDtypeStruct((n, d), F32)] + cast_out_shapes,
        compiler_params=_params(("parallel",)),
        name="swiglu_ffn",
    )(x2d, *proj_args, g_pre.reshape(1, d), wgu, wd, g_post.reshape(1, d), *cast_args)
    return outs[0], tuple(outs[1:])


def _conv_kernel(x_ref, gpre_ref, win_ref, bin_ref, wdw_ref, bdw_ref, lng_ref, lnb_ref, wout_ref, bout_ref,
                 gpost_ref, o_ref, ubuf, cbuf, *, tm, kw):
    d = x_ref.shape[-1]
    row_chunk = 128

    @pl.when(pl.program_id(1) == 0)
    def _():
        ubuf[0:CONV_HALO, :] = jnp.zeros((CONV_HALO, d), F32)

    x = x_ref[0]
    h = _rms(x, gpre_ref[...]).astype(BF16)

    first = CONV_HALO - (kw - 1)
    taps_by_shift = {}
    for j in range(kw):
        shift = (first + j) % SUBLANES
        taps_by_shift.setdefault(shift, []).append((j, first + j - shift))

    for c0 in range(0, d, MXU_COLS):
        cols = slice(c0, c0 + MXU_COLS)
        gcols = slice(d + c0, d + c0 + MXU_COLS)
        a = jnp.dot(h, win_ref[:, cols], preferred_element_type=F32) + bin_ref[:, cols]
        gt = jnp.dot(h, win_ref[:, gcols], preferred_element_type=F32) + bin_ref[:, gcols]
        ubuf[CONV_HALO:CONV_HALO + tm, cols] = a * jax.nn.sigmoid(gt)
        for c in range(c0, c0 + MXU_COLS, LANES):
            cs = slice(c, c + LANES)
            for r0 in range(0, tm, row_chunk):
                win = ubuf[r0:r0 + row_chunk + CONV_HALO, cs]
                acc = jnp.broadcast_to(bdw_ref[:, cs], (row_chunk, LANES))
                for shift, taps in sorted(taps_by_shift.items()):
                    rows = row_chunk + (SUBLANES if shift else 0)
                    part = None
                    for j, start in taps:
                        term = wdw_ref[j:j + 1, cs] * win[start:start + rows]
                        part = term if part is None else part + term
                    acc = acc + part[shift:shift + row_chunk]
                cbuf[r0:r0 + row_chunk, cs] = acc
        ubuf[0:CONV_HALO, cols] = ubuf[tm:tm + CONV_HALO, cols]

    cv = cbuf[...]
    mu = jnp.mean(cv, axis=-1, keepdims=True)
    cen = cv - mu
    var = jnp.mean(cen * cen, axis=-1, keepdims=True)
    un = cen * lax.rsqrt(var + LN_EPS) * lng_ref[...] + lnb_ref[...]
    s = (un * jax.nn.sigmoid(un)).astype(BF16)
    y = jnp.dot(s, wout_ref[...], preferred_element_type=F32) + bout_ref[...]
    o_ref[0] = x + _rms(y, gpost_ref[...])


def _conv_mixer(x, g_pre, w_in, b_in, w_dw, b_dw, ln_g, ln_b, w_out, b_out, g_post):
    b, t, d = x.shape
    kw = w_dw.shape[0]
    tm = ROW_TILE
    assert t % tm == 0 and kw - 1 <= CONV_HALO <= tm and d % LANES == 0
    row = pl.BlockSpec((1, tm, d), lambda bi, ti: (bi, ti, 0))
    vec = lambda a: a.reshape(1, -1)
    return pl.pallas_call(
        functools.partial(_conv_kernel, tm=tm, kw=kw),
        grid=(b, t // tm),
        in_specs=[row, _resident((1, d)), _resident((d, 2 * d)), _resident((1, 2 * d)), _resident((kw, d)),
                  _resident((1, d)), _resident((1, d)), _resident((1, d)), _resident((d, d)), _resident((1, d)),
                  _resident((1, d))],
        out_specs=row,
        out_shape=jax.ShapeDtypeStruct((b, t, d), F32),
        scratch_shapes=[pltpu.VMEM((CONV_HALO + tm, d), F32), pltpu.VMEM((tm, d), F32)],
        compiler_params=_params(("arbitrary", "arbitrary")),
        name="conformer_conv",
    )(x, vec(g_pre), w_in, vec(b_in), w_dw, vec(b_dw), vec(ln_g), vec(ln_b), w_out, vec(b_out), vec(g_post))


def _qkv_kernel(x_ref, gmix_ref, gkv_ref, wqt_ref, wk_ref, wvt_ref, qt_ref, k_ref, vt_ref, km_ref, *, scale):
    x = x_ref[0]
    tm, d = x.shape
    r = x * lax.rsqrt(jnp.mean(x * x, axis=-1, keepdims=True) + RMS_EPS)
    hq = (r * gmix_ref[...]).astype(BF16)
    hkv = (r * gkv_ref[...]).astype(BF16)
    qt_ref[0] = (lax.dot_general(wqt_ref[...], hq, _NT, preferred_element_type=F32) * scale).astype(BF16)
    k = jnp.dot(hkv, wk_ref[...], preferred_element_type=F32)
    k_ref[0] = k.astype(BF16)
    km_ref[0, 0] = jnp.mean(k.reshape(tm // MOBA_BLOCK, MOBA_BLOCK, d), axis=1)
    vt_ref[0] = lax.dot_general(wvt_ref[...], hkv, _NT, preferred_element_type=F32).astype(BF16)


def _qkv(x, g_mix, g_kv, wqt, wk, wvt):
    b, t, d = x.shape
    tm = WIDE_ROW_TILE
    assert t % tm == 0 and tm % MOBA_BLOCK == 0
    per = tm // MOBA_BLOCK
    row = pl.BlockSpec((1, tm, d), lambda bi, ti: (bi, ti, 0))
    col = pl.BlockSpec((1, d, tm), lambda bi, ti: (bi, 0, ti))
    qt, k, vt, km = pl.pallas_call(
        functools.partial(_qkv_kernel, scale=HEAD_DIM ** -0.5 * LOG2_E),
        grid=(b, t // tm),
        in_specs=[row, _resident((1, d)), _resident((1, d)), _resident((d, d)), _resident((d, d)),
                  _resident((d, d))],
        out_specs=[col, row, col, pl.BlockSpec((1, 1, per, d), lambda bi, ti: (bi, ti, 0, 0))],
        out_shape=[jax.ShapeDtypeStruct((b, d, t), BF16), jax.ShapeDtypeStruct((b, t, d), BF16),
                   jax.ShapeDtypeStruct((b, d, t), BF16), jax.ShapeDtypeStruct((b, t // tm, per, d), F32)],
        compiler_params=_params(("parallel", "parallel")),
        name="qkv_proj",
    )(x, g_mix.reshape(1, d), g_kv.reshape(1, d), wqt, wk, wvt)
    return qt, k, vt, km.reshape(b, t // MOBA_BLOCK, d)


def _select_topk(gate, n_idx):
    picked = jnp.zeros(gate.shape, jnp.bool_)
    for _ in range(MOBA_TOPK):
        best = jnp.max(gate, axis=0, keepdims=True)
        first = jnp.min(jnp.where(gate == best, n_idx, gate.shape[0]), axis=0, keepdims=True)
        hit = n_idx == first
        picked = picked | hit
        gate = jnp.where(hit, -jnp.inf, gate)
    return picked


def _attn_kernel(slopes_ref, qt_ref, k_ref, vt_ref, km_ref, sd_ref, own_ref, o_ref, m_s, l_s, mx_s, acc_s, sel_s,
                 sc_s, *, g):
    blk, dh = MOBA_BLOCK, HEAD_DIM
    hg = pl.program_id(1)
    i = pl.program_id(2)
    nb = km_ref.shape[1]
    n_idx = lax.broadcasted_iota(jnp.int32, (nb, blk), 0)
    past = n_idx < i
    ones_rows = jnp.ones((BF16_SUBLANES, blk), BF16)

    heads = [slice(hh * dh, (hh + 1) * dh) for hh in range(g)]
    slopes = [slopes_ref[hg * g + hh] for hh in range(g)]

    def block_rows(n):
        return pl.ds(pl.multiple_of(n * blk, blk), blk)

    def scores_of(rows):
        return [jnp.dot(k_ref[0, rows, hs], qt_ref[0, hs, :], preferred_element_type=F32)
                for hs in heads]

    def values_of(rows):
        return [jnp.concatenate([vt_ref[0, hs, rows], ones_rows], axis=0) for hs in heads]

    own_scores = scores_of(block_rows(i))
    own_values = values_of(block_rows(i))
    first_scores = scores_of(block_rows(0))
    gates = [jnp.dot(km_ref[0, :, hs].astype(BF16), qt_ref[0, hs, :], preferred_element_type=F32)
             for hs in heads]
    init = []
    for hh in range(g):
        picked = _select_topk(jnp.where(past, gates[hh], NEG_INF), n_idx)
        sel = jnp.where(picked & past, 0.0, NEG_INF)
        s = own_scores[hh] - own_ref[hh]
        m0 = jnp.max(s, axis=0, keepdims=True)
        pv = jnp.dot(own_values[hh], jnp.exp2(s - m0).astype(BF16), preferred_element_type=F32)
        u = first_scores[hh] - sd_ref[hh]
        init.append((sel, m0, pv[dh:dh + 1], pv[:dh], u, jnp.max(u, axis=0, keepdims=True)))
    for hh in range(g):
        sel_s[hh], m_s[hh], l_s[hh], acc_s[hh], sc_s[hh], mx_s[hh] = init[hh]

    def past_block(n, carry):
        base = ((i - n) * blk).astype(F32)
        values = values_of(block_rows(n))
        next_scores = scores_of(block_rows(jnp.minimum(n + 1, i - 1)))
        new = []
        for hh in range(g):
            qbias = sel_s[hh, pl.ds(n, 1), :] - slopes[hh] * base
            m_old = m_s[hh]
            m_new = jnp.maximum(m_old, mx_s[hh] + qbias)
            alpha = jnp.exp2(m_old - m_new)
            p = jnp.exp2(sc_s[hh] - (m_new - qbias))
            pv = jnp.dot(values[hh], p.astype(BF16), preferred_element_type=F32)
            u = next_scores[hh] - sd_ref[hh]
            new.append((m_new, alpha * l_s[hh] + pv[dh:dh + 1], alpha * acc_s[hh] + pv[:dh], u,
                        jnp.max(u, axis=0, keepdims=True)))
        for hh in range(g):
            m_s[hh], l_s[hh], acc_s[hh], sc_s[hh], mx_s[hh] = new[hh]
        return carry

    lax.fori_loop(0, i, past_block, 0)

    for hh in range(g):
        o_ref[0, :, hh * dh:(hh + 1) * dh] = (acc_s[hh] / l_s[hh]).T.astype(BF16)


def _alibi_tables(n_heads):
    slopes = 2.0 ** (-8.0 * (np.arange(n_heads) + 1) / n_heads) * LOG2_E
    pos = np.arange(MOBA_BLOCK)
    dist = (pos[None, :] - pos[:, None]).astype(np.float64)
    sd = slopes[:, None, None] * dist
    own = np.where(dist >= 0, sd, -NEG_INF)
    return jnp.asarray(slopes, F32), jnp.asarray(sd, F32), jnp.asarray(own, F32)


def _attention(qt, k, vt, km):
    b, t, d = k.shape
    g, blk = HEADS_PER_STEP, MOBA_BLOCK
    nb = t // blk
    gw = g * HEAD_DIM
    assert t % blk == 0 and d % gw == 0
    slopes, sd, own = _alibi_tables(d // HEAD_DIM)
    qo = pl.BlockSpec((1, blk, gw), lambda bi, hi, ti: (bi, ti, hi))
    table = pl.BlockSpec((g, blk, blk), lambda bi, hi, ti: (hi, 0, 0))
    stat = pltpu.VMEM((g, 1, blk), F32)
    return pl.pallas_call(
        functools.partial(_attn_kernel, g=g),
        grid=(b, d // gw, nb),
        in_specs=[pl.BlockSpec(memory_space=pltpu.SMEM),
                  pl.BlockSpec((1, gw, blk), lambda bi, hi, ti: (bi, hi, ti)),
                  pl.BlockSpec((1, t, gw), lambda bi, hi, ti: (bi, 0, hi)),
                  pl.BlockSpec((1, gw, t), lambda bi, hi, ti: (bi, hi, 0)),
                  pl.BlockSpec((1, nb, gw), lambda bi, hi, ti: (bi, 0, hi)), table, table],
        out_specs=qo,
        out_shape=jax.ShapeDtypeStruct((b, t, d), BF16),
        scratch_shapes=[stat, stat, stat, pltpu.VMEM((g, HEAD_DIM, blk), F32), pltpu.VMEM((g, nb, blk), F32),
                        pltpu.VMEM((g, blk, blk), F32)],
        compiler_params=_params(("parallel", "parallel", "arbitrary")),
        name="moba_attention",
    )(slopes, qt, k, vt, km, sd, own)


def kernel(x, ffn_norm_pre, ffn_norm_post, ffn_w_gate_up, ffn_w_down, mix_norm_pre, mix_norm_post, conv_w_in,
           conv_b_in, conv_w_dw, conv_b_dw, conv_ln_g, conv_ln_b, conv_w_out, conv_b_out, kv_norm, w_kv,
           attn_w_q, attn_w_o):
    b, t, d = x.shape
    depth = ffn_w_gate_up.shape[0]
    n_conv = conv_w_in.shape[0]
    assert depth - n_conv == attn_w_q.shape[0] == 1, "one MoBA layer reading the shared K/V is supported"
    bf = lambda w: w.astype(BF16)
    halves = [(layer, half) for layer in range(depth) for half in range(2)]
    weights = (bf(ffn_w_gate_up[0, 0]), bf(ffn_w_down[0, 0]))

    def ffn(x, layer, half, proj=None):
        nonlocal weights
        nxt = halves.index((layer, half)) + 1
        cast_next = (ffn_w_gate_up, ffn_w_down) + halves[nxt] if nxt < len(halves) else None
        y, cast = _ffn(x.reshape(b * t, d), ffn_norm_pre[layer, half], *weights, ffn_norm_post[layer, half], proj,
                       cast_next)
        weights = cast
        return y.reshape(b, t, d)

    for layer in range(depth):
        x = ffn(x, layer, 0)
        if layer < n_conv:
            a = layer
            x = _conv_mixer(x, mix_norm_pre[layer], bf(conv_w_in[a]), conv_b_in[a], conv_w_dw[a], conv_b_dw[a],
                            conv_ln_g[a], conv_ln_b[a], bf(conv_w_out[a]), conv_b_out[a], mix_norm_post[layer])
            x = ffn(x, layer, 1)
        else:
            j = layer - n_conv
            qt, k, vt, km = _qkv(x, mix_norm_pre[layer], kv_norm, bf(attn_w_q[j].T), bf(w_kv[:, :d]),
                                 bf(w_kv[:, d:].T))
            att = _attention(qt, k, vt, km)
            x = ffn(x, layer, 1, (att.reshape(b * t, d), bf(attn_w_o[j]), mix_norm_post[layer]))
    return x
```

```python
import functools

import numpy as np
import jax
import jax.numpy as jnp
from jax import lax
from jax.experimental import pallas as pl
from jax.experimental.pallas import tpu as pltpu

F32 = jnp.float32
BF16 = jnp.bfloat16

RMS_EPS = 1e-6
LN_EPS = 1e-5
MACARON_WEIGHT = 0.5
NEG_INF = -1e30
LOG2_E = 1.4426950408889634
HEAD_DIM = 128
MOBA_BLOCK = 256
MOBA_TOPK = 3

V7X_VMEM_BYTES = 64 * 1024 * 1024
VMEM_LIMIT_BYTES = V7X_VMEM_BYTES - 8 * 1024 * 1024
LANES = 128
SUBLANES = 8
MXU_COLS = 256
BF16_SUBLANES = 16
CONV_HALO = 32

ROW_TILE = 512
WIDE_ROW_TILE = 1024
FF_CHUNK = 256
HEADS_PER_STEP = 8


_NT = (((1,), (1,)), ((), ()))
_TN_T = (((0,), (1,)), ((), ()))


def _rms(x, g):
    return x * lax.rsqrt(jnp.mean(x * x, axis=-1, keepdims=True) + RMS_EPS) * g


def _resident(shape):
    return pl.BlockSpec(shape, lambda *_: (0,) * len(shape), pipeline_mode=pl.Buffered(1))


def _params(semantics):
    return pltpu.CompilerParams(dimension_semantics=semantics, vmem_limit_bytes=VMEM_LIMIT_BYTES)


def _ffn_kernel(*refs, d_ff, fc, sub, with_proj, n_cast):
    refs = list(refs)
    x_ref = refs.pop(0)
    att_ref, wo_ref, gmix_ref = [refs.pop(0) for _ in range(3)] if with_proj else (None, None, None)
    gpre_ref, wgu_ref, wd_ref, gpost_ref = [refs.pop(0) for _ in range(4)]
    cast_in = [refs.pop(0) for _ in range(n_cast)]
    o_ref = refs.pop(0)
    for src_ref, dst_ref in zip(cast_in, refs):
        dst_ref[...] = src_ref[...].astype(BF16)
    tiles = []
    for r0 in range(0, x_ref.shape[0], sub):
        x = x_ref[r0:r0 + sub, :]
        if with_proj:
            y = jnp.dot(att_ref[r0:r0 + sub, :], wo_ref[...], preferred_element_type=F32)
            x = x + _rms(y, gmix_ref[...])
        tiles.append((r0, x, _rms(x, gpre_ref[...]).astype(BF16)))
    for r0, x, xn in tiles:
        acc = jnp.zeros(x.shape, F32)
        for f in range(d_ff // fc):
            gate = jnp.dot(xn, wgu_ref[:, f * fc:(f + 1) * fc], preferred_element_type=F32)
            up = jnp.dot(xn, wgu_ref[:, d_ff + f * fc:d_ff + (f + 1) * fc], preferred_element_type=F32)
            h = (gate * jax.nn.sigmoid(gate) * up).astype(BF16)
            acc = acc + jnp.dot(h, wd_ref[f * fc:(f + 1) * fc, :], preferred_element_type=F32)
        o_ref[r0:r0 + sub, :] = x + MACARON_WEIGHT * _rms(acc, gpost_ref[...])


def _ffn(x2d, g_pre, wgu, wd, g_post, proj=None, casts=()):
    n, d = x2d.shape
    d_ff = wd.shape[0]
    tm = WIDE_ROW_TILE
    steps = n // tm
    assert n % tm == 0 and tm % ROW_TILE == 0 and d_ff % FF_CHUNK == 0
    row = pl.BlockSpec((tm, d), lambda i: (i, 0))
    proj_specs, proj_args = [], []
    if proj is not None:
        att2d, wo, g_mix = proj
        proj_specs = [row, _resident((d, d)), _resident((1, d))]
        proj_args = [att2d, wo, g_mix.reshape(1, d)]
    cast_specs, cast_out_specs, cast_out_shapes = [], [], []
    for w, lead, col_block in casts:
        rows, cols = w.shape[-2:]
        cb, n_cb = col_block if col_block is not None else (0, 1)
        assert rows % (steps * BF16_SUBLANES) == 0 and cols % n_cb == 0
        slab = (rows // steps, cols // n_cb)
        cast_specs.append(pl.BlockSpec((None,) * len(lead) + slab, lambda i, lead=lead, cb=cb: (*lead, i, cb)))
        cast_out_specs.append(pl.BlockSpec(slab, lambda i: (i, 0)))
        cast_out_shapes.append(jax.ShapeDtypeStruct((rows, cols // n_cb), BF16))
    outs = pl.pallas_call(
        functools.partial(_ffn_kernel, d_ff=d_ff, fc=FF_CHUNK, sub=ROW_TILE, with_proj=proj is not None,
                          n_cast=len(casts)),
        grid=(steps,),
        in_specs=[row] + proj_specs + [_resident((1, d)), _resident((d, 2 * d_ff)), _resident((d_ff, d)),
                                       _resident((1, d))] + cast_specs,
        out_specs=[row] + cast_out_specs,
        out_shape=[jax.ShapeDtypeStruct((n, d), F32)] + cast_out_shapes,
        compiler_params=_params(("parallel",)),
        name="swiglu_ffn",
    )(x2d, *proj_args, g_pre.reshape(1, d), wgu, wd, g_post.reshape(1, d), *[w for w, _, _ in casts])
    return outs[0], list(outs[1:])


def _conv_kernel(x_ref, gpre_ref, win_ref, bin_ref, wdw_ref, bdw_ref, lng_ref, lnb_ref, wout_ref, bout_ref,
                 gpost_ref, o_ref, ubuf, cbuf, *, tm, kw):
    d = x_ref.shape[-1]
    row_chunk = 128

    @pl.when(pl.program_id(1) == 0)
    def _():
        ubuf[0:CONV_HALO, :] = jnp.zeros((CONV_HALO, d), F32)

    x = x_ref[0]
    h = _rms(x, gpre_ref[...]).astype(BF16)

    first = CONV_HALO - (kw - 1)
    taps_by_shift = {}
    for j in range(kw):
        shift = (first + j) % SUBLANES
        taps_by_shift.setdefault(shift, []).append((j, first + j - shift))

    for c0 in range(0, d, MXU_COLS):
        cols = slice(c0, c0 + MXU_COLS)
        gcols = slice(d + c0, d + c0 + MXU_COLS)
        a = jnp.dot(h, win_ref[:, cols], preferred_element_type=F32) + bin_ref[:, cols]
        gt = jnp.dot(h, win_ref[:, gcols], preferred_element_type=F32) + bin_ref[:, gcols]
        ubuf[CONV_HALO:CONV_HALO + tm, cols] = a * jax.nn.sigmoid(gt)
        for c in range(c0, c0 + MXU_COLS, LANES):
            cs = slice(c, c + LANES)
            for r0 in range(0, tm, row_chunk):
                win = ubuf[r0:r0 + row_chunk + CONV_HALO, cs]
                acc = jnp.broadcast_to(bdw_ref[:, cs], (row_chunk, LANES))
                for shift, taps in sorted(taps_by_shift.items()):
                    rows = row_chunk + (SUBLANES if shift else 0)
                    part = None
                    for j, start in taps:
                        term = wdw_ref[j:j + 1, cs] * win[start:start + rows]
                        part = term if part is None else part + term
                    acc = acc + part[shift:shift + row_chunk]
                cbuf[r0:r0 + row_chunk, cs] = acc
        ubuf[0:CONV_HALO, cols] = ubuf[tm:tm + CONV_HALO, cols]

    cv = cbuf[...]
    mu = jnp.mean(cv, axis=-1, keepdims=True)
    cen = cv - mu
    var = jnp.mean(cen * cen, axis=-1, keepdims=True)
    un = cen * lax.rsqrt(var + LN_EPS) * lng_ref[...] + lnb_ref[...]
    s = (un * jax.nn.sigmoid(un)).astype(BF16)
    y = jnp.dot(s, wout_ref[...], preferred_element_type=F32) + bout_ref[...]
    o_ref[0] = x + _rms(y, gpost_ref[...])


def _conv_mixer(x, g_pre, w_in, b_in, w_dw, b_dw, ln_g, ln_b, w_out, b_out, g_post):
    b, t, d = x.shape
    kw = w_dw.shape[0]
    tm = ROW_TILE
    assert t % tm == 0 and kw - 1 <= CONV_HALO <= tm and d % LANES == 0
    row = pl.BlockSpec((1, tm, d), lambda bi, ti: (bi, ti, 0))
    vec = lambda a: a.reshape(1, -1)
    return pl.pallas_call(
        functools.partial(_conv_kernel, tm=tm, kw=kw),
        grid=(b, t // tm),
        in_specs=[row, _resident((1, d)), _resident((d, 2 * d)), _resident((1, 2 * d)), _resident((kw, d)),
                  _resident((1, d)), _resident((1, d)), _resident((1, d)), _resident((d, d)), _resident((1, d)),
                  _resident((1, d))],
        out_specs=row,
        out_shape=jax.ShapeDtypeStruct((b, t, d), F32),
        scratch_shapes=[pltpu.VMEM((CONV_HALO + tm, d), F32), pltpu.VMEM((tm, d), F32)],
        compiler_params=_params(("arbitrary", "arbitrary")),
        name="conformer_conv",
    )(x, vec(g_pre), w_in, vec(b_in), w_dw, vec(b_dw), vec(ln_g), vec(ln_b), w_out, vec(b_out), vec(g_post))


def _qkv_kernel(x_ref, gmix_ref, gkv_ref, wq_ref, wk_ref, wv_ref, qt_ref, k_ref, vt_ref, km_ref, *, scale):
    x = x_ref[0]
    tm, d = x.shape
    r = x * lax.rsqrt(jnp.mean(x * x, axis=-1, keepdims=True) + RMS_EPS)
    hq = (r * gmix_ref[...]).astype(BF16)
    hkv = (r * gkv_ref[...]).astype(BF16)
    qt_ref[0] = (lax.dot_general(wq_ref[...], hq, _TN_T, preferred_element_type=F32) * scale).astype(BF16)
    k = jnp.dot(hkv, wk_ref[...], preferred_element_type=F32)
    k_ref[0] = k.astype(BF16)
    km_ref[0, 0] = jnp.mean(k.reshape(tm // MOBA_BLOCK, MOBA_BLOCK, d), axis=1)
    vt_ref[0] = lax.dot_general(wv_ref[...], hkv, _TN_T, preferred_element_type=F32).astype(BF16)


def _qkv(x, g_mix, g_kv, wq, wk, wv):
    b, t, d = x.shape
    tm = WIDE_ROW_TILE
    assert t % tm == 0 and tm % MOBA_BLOCK == 0
    per = tm // MOBA_BLOCK
    row = pl.BlockSpec((1, tm, d), lambda bi, ti: (bi, ti, 0))
    col = pl.BlockSpec((1, d, tm), lambda bi, ti: (bi, 0, ti))
    qt, k, vt, km = pl.pallas_call(
        functools.partial(_qkv_kernel, scale=HEAD_DIM ** -0.5 * LOG2_E),
        grid=(b, t // tm),
        in_specs=[row, _resident((1, d)), _resident((1, d)), _resident((d, d)), _resident((d, d)),
                  _resident((d, d))],
        out_specs=[col, row, col, pl.BlockSpec((1, 1, per, d), lambda bi, ti: (bi, ti, 0, 0))],
        out_shape=[jax.ShapeDtypeStruct((b, d, t), BF16), jax.ShapeDtypeStruct((b, t, d), BF16),
                   jax.ShapeDtypeStruct((b, d, t), BF16), jax.ShapeDtypeStruct((b, t // tm, per, d), F32)],
        compiler_params=_params(("parallel", "parallel")),
        name="qkv_proj",
    )(x, g_mix.reshape(1, d), g_kv.reshape(1, d), wq, wk, wv)
    return qt, k, vt, km.reshape(b, t // MOBA_BLOCK, d)


def _select_topk(gate, n_idx):
    picked = jnp.zeros(gate.shape, jnp.bool_)
    for _ in range(MOBA_TOPK):
        best = jnp.max(gate, axis=0, keepdims=True)
        first = jnp.min(jnp.where(gate == best, n_idx, gate.shape[0]), axis=0, keepdims=True)
        hit = n_idx == first
        picked = picked | hit
        gate = jnp.where(hit, -jnp.inf, gate)
    return picked


def _attn_kernel(slopes_ref, qt_ref, k_ref, vt_ref, km_ref, sd_ref, own_ref, o_ref, m_s, l_s, mx_s, acc_s, sel_s,
                 sc_s, *, g):
    blk, dh = MOBA_BLOCK, HEAD_DIM
    hg = pl.program_id(1)
    i = pl.program_id(2)
    nb = km_ref.shape[1]
    n_idx = lax.broadcasted_iota(jnp.int32, (nb, blk), 0)
    past = n_idx < i
    ones_rows = jnp.ones((BF16_SUBLANES, blk), BF16)

    heads = [slice(hh * dh, (hh + 1) * dh) for hh in range(g)]
    slopes = [slopes_ref[hg * g + hh] for hh in range(g)]

    def block_rows(n):
        return pl.ds(pl.multiple_of(n * blk, blk), blk)

    def scores_of(rows):
        return [jnp.dot(k_ref[0, rows, hs], qt_ref[0, hs, :], preferred_element_type=F32)
                for hs in heads]

    def values_of(rows):
        return [jnp.concatenate([vt_ref[0, hs, rows], ones_rows], axis=0) for hs in heads]

    own_scores = scores_of(block_rows(i))
    own_values = values_of(block_rows(i))
    first_scores = scores_of(block_rows(0))
    gates = [jnp.dot(km_ref[0, :, hs].astype(BF16), qt_ref[0, hs, :], preferred_element_type=F32)
             for hs in heads]
    init = []
    for hh in range(g):
        picked = _select_topk(jnp.where(past, gates[hh], NEG_INF), n_idx)
        sel = jnp.where(picked & past, 0.0, NEG_INF)
        s = own_scores[hh] - own_ref[hh]
        m0 = jnp.max(s, axis=0, keepdims=True)
        pv = jnp.dot(own_values[hh], jnp.exp2(s - m0).astype(BF16), preferred_element_type=F32)
        u = first_scores[hh] - sd_ref[hh]
        init.append((sel, m0, pv[dh:dh + 1], pv[:dh], u, jnp.max(u, axis=0, keepdims=True)))
    for hh in range(g):
        sel_s[hh], m_s[hh], l_s[hh], acc_s[hh], sc_s[hh], mx_s[hh] = init[hh]

    def past_block(n, carry):
        base = ((i - n) * blk).astype(F32)
        values = values_of(block_rows(n))
        next_scores = scores_of(block_rows(jnp.minimum(n + 1, i - 1)))
        new = []
        for hh in range(g):
            qbias = sel_s[hh, pl.ds(n, 1), :] - slopes[hh] * base
            m_old = m_s[hh]
            m_new = jnp.maximum(m_old, mx_s[hh] + qbias)
            alpha = jnp.exp2(m_old - m_new)
            p = jnp.exp2(sc_s[hh] - (m_new - qbias))
            pv = jnp.dot(values[hh], p.astype(BF16), preferred_element_type=F32)
            u = next_scores[hh] - sd_ref[hh]
            new.append((m_new, alpha * l_s[hh] + pv[dh:dh + 1], alpha * acc_s[hh] + pv[:dh], u,
                        jnp.max(u, axis=0, keepdims=True)))
        for hh in range(g):
            m_s[hh], l_s[hh], acc_s[hh], sc_s[hh], mx_s[hh] = new[hh]
        return carry

    lax.fori_loop(0, i, past_block, 0)

    for hh in range(g):
        o_ref[0, :, hh * dh:(hh + 1) * dh] = (acc_s[hh] / l_s[hh]).T.astype(BF16)


def _alibi_tables(n_heads):
    slopes = 2.0 ** (-8.0 * (np.arange(n_heads) + 1) / n_heads) * LOG2_E
    pos = np.arange(MOBA_BLOCK)
    dist = (pos[None, :] - pos[:, None]).astype(np.float64)
    sd = slopes[:, None, None] * dist
    own = np.where(dist >= 0, sd, -NEG_INF)
    return jnp.asarray(slopes, F32), jnp.asarray(sd, F32), jnp.asarray(own, F32)


def _attention(qt, k, vt, km):
    b, t, d = k.shape
    g, blk = HEADS_PER_STEP, MOBA_BLOCK
    nb = t // blk
    gw = g * HEAD_DIM
    assert t % blk == 0 and d % gw == 0
    slopes, sd, own = _alibi_tables(d // HEAD_DIM)
    qo = pl.BlockSpec((1, blk, gw), lambda bi, hi, ti: (bi, ti, hi))
    table = pl.BlockSpec((g, blk, blk), lambda bi, hi, ti: (hi, 0, 0))
    stat = pltpu.VMEM((g, 1, blk), F32)
    return pl.pallas_call(
        functools.partial(_attn_kernel, g=g),
        grid=(b, d // gw, nb),
        in_specs=[pl.BlockSpec(memory_space=pltpu.SMEM),
                  pl.BlockSpec((1, gw, blk), lambda bi, hi, ti: (bi, hi, ti)),
                  pl.BlockSpec((1, t, gw), lambda bi, hi, ti: (bi, 0, hi)),
                  pl.BlockSpec((1, gw, t), lambda bi, hi, ti: (bi, hi, 0)),
                  pl.BlockSpec((1, nb, gw), lambda bi, hi, ti: (bi, 0, hi)), table, table],
        out_specs=qo,
        out_shape=jax.ShapeDtypeStruct((b, t, d), BF16),
        scratch_shapes=[stat, stat, stat, pltpu.VMEM((g, HEAD_DIM, blk), F32), pltpu.VMEM((g, nb, blk), F32),
                        pltpu.VMEM((g, blk, blk), F32)],
        compiler_params=_params(("parallel", "parallel", "arbitrary")),
        name="moba_attention",
    )(slopes, qt, k, vt, km, sd, own)


def kernel(x, ffn_norm_pre, ffn_norm_post, ffn_w_gate_up, ffn_w_down, mix_norm_pre, mix_norm_post, conv_w_in,
           conv_b_in, conv_w_dw, conv_b_dw, conv_ln_g, conv_ln_b, conv_w_out, conv_b_out, kv_norm, w_kv,
           attn_w_q, attn_w_o):
    b, t, d = x.shape
    depth = ffn_w_gate_up.shape[0]
    n_conv = conv_w_in.shape[0]
    assert depth - n_conv == attn_w_q.shape[0] == 1, "one MoBA layer reading the shared K/V is supported"
    bf = lambda w: w.astype(BF16)
    halves = [(layer, half) for layer in range(depth) for half in range(2)]
    weights = [bf(ffn_w_gate_up[0, 0]), bf(ffn_w_down[0, 0])]

    def ffn(x, layer, half, proj=None, mixer_casts=()):
        nonlocal weights
        nxt = halves.index((layer, half)) + 1
        casts = [(w, halves[nxt], None) for w in (ffn_w_gate_up, ffn_w_down)] if nxt < len(halves) else []
        y, casted = _ffn(x.reshape(b * t, d), ffn_norm_pre[layer, half], *weights, ffn_norm_post[layer, half], proj,
                         casts + list(mixer_casts))
        weights = casted[:len(casts)]
        return y.reshape(b, t, d), casted[len(casts):]

    for layer in range(depth):
        if layer < n_conv:
            a = layer
            x, (w_in, w_out) = ffn(x, layer, 0, mixer_casts=[(conv_w_in, (a,), None), (conv_w_out, (a,), None)])
            x = _conv_mixer(x, mix_norm_pre[layer], w_in, conv_b_in[a], conv_w_dw[a], conv_b_dw[a],
                            conv_ln_g[a], conv_ln_b[a], w_out, conv_b_out[a], mix_norm_post[layer])
            x, _ = ffn(x, layer, 1)
        else:
            j = layer - n_conv
            x, (wq, wk, wv, wo) = ffn(x, layer, 0, mixer_casts=[(attn_w_q, (j,), None), (w_kv, (), (0, 2)),
                                                                (w_kv, (), (1, 2)), (attn_w_o, (j,), None)])
            qt, k, vt, km = _qkv(x, mix_norm_pre[layer], kv_norm, wq, wk, wv)
            att = _attention(qt, k, vt, km)
            x, _ = ffn(x, layer, 1, (att.reshape(b * t, d), wo, mix_norm_post[layer]))
    return x
```

```python
import functools

import numpy as np
import jax
import jax.numpy as jnp
from jax import lax
from jax.experimental import pallas as pl
from jax.experimental.pallas import tpu as pltpu

F32 = jnp.float32
BF16 = jnp.bfloat16

RMS_EPS = 1e-6
LN_EPS = 1e-5
MACARON_WEIGHT = 0.5
NEG_INF = -1e30
LOG2_E = 1.4426950408889634
HEAD_DIM = 128
MOBA_BLOCK = 256
MOBA_TOPK = 3

V7X_VMEM_BYTES = 64 * 1024 * 1024
VMEM_LIMIT_BYTES = V7X_VMEM_BYTES - 8 * 1024 * 1024
LANES = 128
SUBLANES = 8
MXU_COLS = 256
BF16_SUBLANES = 16
CONV_HALO = 32

ROW_TILE = 512
WIDE_ROW_TILE = 1024
FF_CHUNK = 256
STAGE_SLABS = 16
HEADS_PER_STEP = 8


_NT = (((1,), (1,)), ((), ()))
_TN_T = (((0,), (1,)), ((), ()))


def _rms(x, g):
    return x * lax.rsqrt(jnp.mean(x * x, axis=-1, keepdims=True) + RMS_EPS) * g


def _resident(shape):
    return pl.BlockSpec(shape, lambda *_: (0,) * len(shape), pipeline_mode=pl.Buffered(1))


def _params(semantics):
    return pltpu.CompilerParams(dimension_semantics=semantics, vmem_limit_bytes=VMEM_LIMIT_BYTES)


def _stage_weight(src_hbm, lead, dst, stage, sem):
    slab = stage.shape[1]
    n_slabs = dst.shape[0] // slab

    def copy(c):
        return pltpu.make_async_copy(src_hbm.at[(*lead, pl.ds(c * slab, slab))], stage.at[c % 2], sem.at[c % 2])

    copy(0).start()
    for c in range(n_slabs):
        if c + 1 < n_slabs:
            copy(c + 1).start()
        copy(c).wait()
        dst[c * slab:(c + 1) * slab, :] = stage[c % 2].astype(BF16)


def _ffn_kernel(*refs, d_ff, fc, sub, with_proj, n_cast, staged):
    refs = list(refs)
    x_ref = refs.pop(0)
    att_ref, wo_ref, gmix_ref = [refs.pop(0) for _ in range(3)] if with_proj else (None, None, None)
    gpre_ref, wgu_ref, wd_ref, gpost_ref = [refs.pop(0) for _ in range(4)]
    cast_in = [refs.pop(0) for _ in range(n_cast)]
    o_ref = refs.pop(0)
    if staged is not None:
        wgu_hbm, wd_hbm = wgu_ref, wd_ref
        sem, stage_d, stage_gu, wd_ref, wgu_ref = [refs.pop() for _ in range(5)]

        @pl.when(pl.program_id(0) == 0)
        def _():
            _stage_weight(wgu_hbm, staged, wgu_ref, stage_gu, sem)
            _stage_weight(wd_hbm, staged, wd_ref, stage_d, sem)

    for src_ref, dst_ref in zip(cast_in, refs):
        dst_ref[...] = src_ref[...].astype(BF16)
    tiles = []
    for r0 in range(0, x_ref.shape[0], sub):
        x = x_ref[r0:r0 + sub, :]
        if with_proj:
            y = jnp.dot(att_ref[r0:r0 + sub, :], wo_ref[...], preferred_element_type=F32)
            x = x + _rms(y, gmix_ref[...])
        tiles.append((r0, x, _rms(x, gpre_ref[...]).astype(BF16)))
    for r0, x, xn in tiles:
        acc = jnp.zeros(x.shape, F32)
        for f in range(d_ff // fc):
            gate = jnp.dot(xn, wgu_ref[:, f * fc:(f + 1) * fc], preferred_element_type=F32)
            up = jnp.dot(xn, wgu_ref[:, d_ff + f * fc:d_ff + (f + 1) * fc], preferred_element_type=F32)
            h = (gate * jax.nn.sigmoid(gate) * up).astype(BF16)
            acc = acc + jnp.dot(h, wd_ref[f * fc:(f + 1) * fc, :], preferred_element_type=F32)
        o_ref[r0:r0 + sub, :] = x + MACARON_WEIGHT * _rms(acc, gpost_ref[...])


def _ffn(x2d, g_pre, wgu, wd, g_post, proj=None, casts=(), staged=None):
    n, d = x2d.shape
    d_ff = wd.shape[-2]
    tm = WIDE_ROW_TILE
    steps = n // tm
    assert n % tm == 0 and tm % ROW_TILE == 0 and d_ff % FF_CHUNK == 0
    row = pl.BlockSpec((tm, d), lambda i: (i, 0))
    proj_specs, proj_args = [], []
    if proj is not None:
        att2d, wo, g_mix = proj
        proj_specs = [row, _resident((d, d)), _resident((1, d))]
        proj_args = [att2d, wo, g_mix.reshape(1, d)]
    cast_specs, cast_out_specs, cast_out_shapes = [], [], []
    for w, lead, col_block in casts:
        rows, cols = w.shape[-2:]
        cb, n_cb = col_block if col_block is not None else (0, 1)
        assert rows % (steps * BF16_SUBLANES) == 0 and cols % n_cb == 0
        slab = (rows // steps, cols // n_cb)
        cast_specs.append(pl.BlockSpec((None,) * len(lead) + slab, lambda i, lead=lead, cb=cb: (*lead, i, cb)))
        cast_out_specs.append(pl.BlockSpec(slab, lambda i: (i, 0)))
        cast_out_shapes.append(jax.ShapeDtypeStruct((rows, cols // n_cb), BF16))
    if staged is None:
        weight_specs, scratch = [_resident((d, 2 * d_ff)), _resident((d_ff, d))], []
    else:
        assert d % STAGE_SLABS == 0 and d_ff % STAGE_SLABS == 0
        weight_specs = [pl.BlockSpec(memory_space=pl.ANY)] * 2
        scratch = [pltpu.VMEM((d, 2 * d_ff), BF16), pltpu.VMEM((d_ff, d), BF16),
                   pltpu.VMEM((2, d // STAGE_SLABS, 2 * d_ff), F32), pltpu.VMEM((2, d_ff // STAGE_SLABS, d), F32),
                   pltpu.SemaphoreType.DMA((2,))]
    outs = pl.pallas_call(
        functools.partial(_ffn_kernel, d_ff=d_ff, fc=FF_CHUNK, sub=ROW_TILE, with_proj=proj is not None,
                          n_cast=len(casts), staged=staged),
        grid=(steps,),
        in_specs=[row] + proj_specs + [_resident((1, d))] + weight_specs + [_resident((1, d))] + cast_specs,
        out_specs=[row] + cast_out_specs,
        out_shape=[jax.ShapeDtypeStruct((n, d), F32)] + cast_out_shapes,
        scratch_shapes=scratch,
        compiler_params=_params(("arbitrary",) if staged is not None else ("parallel",)),
        name="swiglu_ffn",
    )(x2d, *proj_args, g_pre.reshape(1, d), wgu, wd, g_post.reshape(1, d), *[w for w, _, _ in casts])
    return outs[0], list(outs[1:])


def _conv_kernel(x_ref, gpre_ref, win_ref, bin_ref, wdw_ref, bdw_ref, lng_ref, lnb_ref, wout_ref, bout_ref,
                 gpost_ref, o_ref, ubuf, cbuf, *, tm, kw):
    d = x_ref.shape[-1]
    row_chunk = 128

    @pl.when(pl.program_id(1) == 0)
    def _():
        ubuf[0:CONV_HALO, :] = jnp.zeros((CONV_HALO, d), F32)

    x = x_ref[0]
    h = _rms(x, gpre_ref[...]).astype(BF16)

    first = CONV_HALO - (kw - 1)
    taps_by_shift = {}
    for j in range(kw):
        shift = (first + j) % SUBLANES
        taps_by_shift.setdefault(shift, []).append((j, first + j - shift))

    for c0 in range(0, d, MXU_COLS):
        cols = slice(c0, c0 + MXU_COLS)
        gcols = slice(d + c0, d + c0 + MXU_COLS)
        a = jnp.dot(h, win_ref[:, cols], preferred_element_type=F32) + bin_ref[:, cols]
        gt = jnp.dot(h, win_ref[:, gcols], preferred_element_type=F32) + bin_ref[:, gcols]
        ubuf[CONV_HALO:CONV_HALO + tm, cols] = a * jax.nn.sigmoid(gt)
        for c in range(c0, c0 + MXU_COLS, LANES):
            cs = slice(c, c + LANES)
            for r0 in range(0, tm, row_chunk):
                win = ubuf[r0:r0 + row_chunk + CONV_HALO, cs]
                acc = jnp.broadcast_to(bdw_ref[:, cs], (row_chunk, LANES))
                for shift, taps in sorted(taps_by_shift.items()):
                    rows = row_chunk + (SUBLANES if shift else 0)
                    part = None
                    for j, start in taps:
                        term = wdw_ref[j:j + 1, cs] * win[start:start + rows]
                        part = term if part is None else part + term
                    acc = acc + part[shift:shift + row_chunk]
                cbuf[r0:r0 + row_chunk, cs] = acc
        ubuf[0:CONV_HALO, cols] = ubuf[tm:tm + CONV_HALO, cols]

    cv = cbuf[...]
    mu = jnp.mean(cv, axis=-1, keepdims=True)
    cen = cv - mu
    var = jnp.mean(cen * cen, axis=-1, keepdims=True)
    un = cen * lax.rsqrt(var + LN_EPS) * lng_ref[...] + lnb_ref[...]
    s = (un * jax.nn.sigmoid(un)).astype(BF16)
    y = jnp.dot(s, wout_ref[...], preferred_element_type=F32) + bout_ref[...]
    o_ref[0] = x + _rms(y, gpost_ref[...])


def _conv_mixer(x, g_pre, w_in, b_in, w_dw, b_dw, ln_g, ln_b, w_out, b_out, g_post):
    b, t, d = x.shape
    kw = w_dw.shape[0]
    tm = ROW_TILE
    assert t % tm == 0 and kw - 1 <= CONV_HALO <= tm and d % LANES == 0
    row = pl.BlockSpec((1, tm, d), lambda bi, ti: (bi, ti, 0))
    vec = lambda a: a.reshape(1, -1)
    return pl.pallas_call(
        functools.partial(_conv_kernel, tm=tm, kw=kw),
        grid=(b, t // tm),
        in_specs=[row, _resident((1, d)), _resident((d, 2 * d)), _resident((1, 2 * d)), _resident((kw, d)),
                  _resident((1, d)), _resident((1, d)), _resident((1, d)), _resident((d, d)), _resident((1, d)),
                  _resident((1, d))],
        out_specs=row,
        out_shape=jax.ShapeDtypeStruct((b, t, d), F32),
        scratch_shapes=[pltpu.VMEM((CONV_HALO + tm, d), F32), pltpu.VMEM((tm, d), F32)],
        compiler_params=_params(("arbitrary", "arbitrary")),
        name="conformer_conv",
    )(x, vec(g_pre), w_in, vec(b_in), w_dw, vec(b_dw), vec(ln_g), vec(ln_b), w_out, vec(b_out), vec(g_post))


def _qkv_kernel(x_ref, gmix_ref, gkv_ref, wq_ref, wk_ref, wv_ref, qt_ref, k_ref, vt_ref, km_ref, *, scale):
    x = x_ref[0]
    tm, d = x.shape
    r = x * lax.rsqrt(jnp.mean(x * x, axis=-1, keepdims=True) + RMS_EPS)
    hq = (r * gmix_ref[...]).astype(BF16)
    hkv = (r * gkv_ref[...]).astype(BF16)
    qt_ref[0] = (lax.dot_general(wq_ref[...], hq, _TN_T, preferred_element_type=F32) * scale).astype(BF16)
    k = jnp.dot(hkv, wk_ref[...], preferred_element_type=F32)
    k_ref[0] = k.astype(BF16)
    km_ref[0, 0] = jnp.mean(k.reshape(tm // MOBA_BLOCK, MOBA_BLOCK, d), axis=1)
    vt_ref[0] = lax.dot_general(wv_ref[...], hkv, _TN_T, preferred_element_type=F32).astype(BF16)


def _qkv(x, g_mix, g_kv, wq, wk, wv):
    b, t, d = x.shape
    tm = WIDE_ROW_TILE
    assert t % tm == 0 and tm % MOBA_BLOCK == 0
    per = tm // MOBA_BLOCK
    row = pl.BlockSpec((1, tm, d), lambda bi, ti: (bi, ti, 0))
    col = pl.BlockSpec((1, d, tm), lambda bi, ti: (bi, 0, ti))
    qt, k, vt, km = pl.pallas_call(
        functools.partial(_qkv_kernel, scale=HEAD_DIM ** -0.5 * LOG2_E),
        grid=(b, t // tm),
        in_specs=[row, _resident((1, d)), _resident((1, d)), _resident((d, d)), _resident((d, d)),
                  _resident((d, d))],
        out_specs=[col, row, col, pl.BlockSpec((1, 1, per, d), lambda bi, ti: (bi, ti, 0, 0))],
        out_shape=[jax.ShapeDtypeStruct((b, d, t), BF16), jax.ShapeDtypeStruct((b, t, d), BF16),
                   jax.ShapeDtypeStruct((b, d, t), BF16), jax.ShapeDtypeStruct((b, t // tm, per, d), F32)],
        compiler_params=_params(("parallel", "parallel")),
        name="qkv_proj",
    )(x, g_mix.reshape(1, d), g_kv.reshape(1, d), wq, wk, wv)
    return qt, k, vt, km.reshape(b, t // MOBA_BLOCK, d)


def _select_topk(gate, n_idx):
    picked = jnp.zeros(gate.shape, jnp.bool_)
    for _ in range(MOBA_TOPK):
        best = jnp.max(gate, axis=0, keepdims=True)
        first = jnp.min(jnp.where(gate == best, n_idx, gate.shape[0]), axis=0, keepdims=True)
        hit = n_idx == first
        picked = picked | hit
        gate = jnp.where(hit, -jnp.inf, gate)
    return picked


def _attn_kernel(slopes_ref, qt_ref, k_ref, vt_ref, km_ref, sd_ref, own_ref, o_ref, m_s, l_s, mx_s, acc_s, sel_s,
                 sc_s, *, g):
    blk, dh = MOBA_BLOCK, HEAD_DIM
    hg = pl.program_id(1)
    i = pl.program_id(2)
    nb = km_ref.shape[1]
    n_idx = lax.broadcasted_iota(jnp.int32, (nb, blk), 0)
    past = n_idx < i
    ones_rows = jnp.ones((BF16_SUBLANES, blk), BF16)

    heads = [slice(hh * dh, (hh + 1) * dh) for hh in range(g)]
    slopes = [slopes_ref[hg * g + hh] for hh in range(g)]

    def block_rows(n):
        return pl.ds(pl.multiple_of(n * blk, blk), blk)

    def scores_of(rows):
        return [jnp.dot(k_ref[0, rows, hs], qt_ref[0, hs, :], preferred_element_type=F32)
                for hs in heads]

    def values_of(rows):
        return [jnp.concatenate([vt_ref[0, hs, rows], ones_rows], axis=0) for hs in heads]

    own_scores = scores_of(block_rows(i))
    own_values = values_of(block_rows(i))
    first_scores = scores_of(block_rows(0))
    gates = [jnp.dot(km_ref[0, :, hs].astype(BF16), qt_ref[0, hs, :], preferred_element_type=F32)
             for hs in heads]
    init = []
    for hh in range(g):
        picked = _select_topk(jnp.where(past, gates[hh], NEG_INF), n_idx)
        sel = jnp.where(picked & past, 0.0, NEG_INF)
        s = own_scores[hh] - own_ref[hh]
        m0 = jnp.max(s, axis=0, keepdims=True)
        pv = jnp.dot(own_values[hh], jnp.exp2(s - m0).astype(BF16), preferred_element_type=F32)
        u = first_scores[hh] - sd_ref[hh]
        init.append((sel, m0, pv[dh:dh + 1], pv[:dh], u, jnp.max(u, axis=0, keepdims=True)))
    for hh in range(g):
        sel_s[hh], m_s[hh], l_s[hh], acc_s[hh], sc_s[hh], mx_s[hh] = init[hh]

    def past_block(n, carry):
        base = ((i - n) * blk).astype(F32)
        values = values_of(block_rows(n))
        next_scores = scores_of(block_rows(jnp.minimum(n + 1, i - 1)))
        new = []
        for hh in range(g):
            qbias = sel_s[hh, pl.ds(n, 1), :] - slopes[hh] * base
            m_old = m_s[hh]
            m_new = jnp.maximum(m_old, mx_s[hh] + qbias)
            alpha = jnp.exp2(m_old - m_new)
            p = jnp.exp2(sc_s[hh] - (m_new - qbias))
            pv = jnp.dot(values[hh], p.astype(BF16), preferred_element_type=F32)
            u = next_scores[hh] - sd_ref[hh]
            new.append((m_new, alpha * l_s[hh] + pv[dh:dh + 1], alpha * acc_s[hh] + pv[:dh], u,
                        jnp.max(u, axis=0, keepdims=True)))
        for hh in range(g):
            m_s[hh], l_s[hh], acc_s[hh], sc_s[hh], mx_s[hh] = new[hh]
        return carry

    lax.fori_loop(0, i, past_block, 0)

    for hh in range(g):
        o_ref[0, :, hh * dh:(hh + 1) * dh] = (acc_s[hh] / l_s[hh]).T.astype(BF16)


def _alibi_tables(n_heads):
    slopes = 2.0 ** (-8.0 * (np.arange(n_heads) + 1) / n_heads) * LOG2_E
    pos = np.arange(MOBA_BLOCK)
    dist = (pos[None, :] - pos[:, None]).astype(np.float64)
    sd = slopes[:, None, None] * dist
    own = np.where(dist >= 0, sd, -NEG_INF)
    return jnp.asarray(slopes, F32), jnp.asarray(sd, F32), jnp.asarray(own, F32)


def _attention(qt, k, vt, km):
    b, t, d = k.shape
    g, blk = HEADS_PER_STEP, MOBA_BLOCK
    nb = t // blk
    gw = g * HEAD_DIM
    assert t % blk == 0 and d % gw == 0
    slopes, sd, own = _alibi_tables(d // HEAD_DIM)
    qo = pl.BlockSpec((1, blk, gw), lambda bi, hi, ti: (bi, ti, hi))
    table = pl.BlockSpec((g, blk, blk), lambda bi, hi, ti: (hi, 0, 0))
    stat = pltpu.VMEM((g, 1, blk), F32)
    return pl.pallas_call(
        functools.partial(_attn_kernel, g=g),
        grid=(b, d // gw, nb),
        in_specs=[pl.BlockSpec(memory_space=pltpu.SMEM),
                  pl.BlockSpec((1, gw, blk), lambda bi, hi, ti: (bi, hi, ti)),
                  pl.BlockSpec((1, t, gw), lambda bi, hi, ti: (bi, 0, hi)),
                  pl.BlockSpec((1, gw, t), lambda bi, hi, ti: (bi, hi, 0)),
                  pl.BlockSpec((1, nb, gw), lambda bi, hi, ti: (bi, 0, hi)), table, table],
        out_specs=qo,
        out_shape=jax.ShapeDtypeStruct((b, t, d), BF16),
        scratch_shapes=[stat, stat, stat, pltpu.VMEM((g, HEAD_DIM, blk), F32), pltpu.VMEM((g, nb, blk), F32),
                        pltpu.VMEM((g, blk, blk), F32)],
        compiler_params=_params(("parallel", "parallel", "arbitrary")),
        name="moba_attention",
    )(slopes, qt, k, vt, km, sd, own)


def kernel(x, ffn_norm_pre, ffn_norm_post, ffn_w_gate_up, ffn_w_down, mix_norm_pre, mix_norm_post, conv_w_in,
           conv_b_in, conv_w_dw, conv_b_dw, conv_ln_g, conv_ln_b, conv_w_out, conv_b_out, kv_norm, w_kv,
           attn_w_q, attn_w_o):
    b, t, d = x.shape
    depth = ffn_w_gate_up.shape[0]
    n_conv = conv_w_in.shape[0]
    assert depth - n_conv == attn_w_q.shape[0] == 1, "one MoBA layer reading the shared K/V is supported"
    halves = [(layer, half) for layer in range(depth) for half in range(2)]
    weights = None

    def ffn(x, layer, half, proj=None, mixer_casts=()):
        nonlocal weights
        nxt = halves.index((layer, half)) + 1
        casts = [(w, halves[nxt], None) for w in (ffn_w_gate_up, ffn_w_down)] if nxt < len(halves) else []
        own = (ffn_w_gate_up, ffn_w_down) if weights is None else weights
        y, casted = _ffn(x.reshape(b * t, d), ffn_norm_pre[layer, half], *own, ffn_norm_post[layer, half], proj,
                         casts + list(mixer_casts), staged=(layer, half) if weights is None else None)
        weights = casted[:len(casts)]
        return y.reshape(b, t, d), casted[len(casts):]

    for layer in range(depth):
        if layer < n_conv:
            a = layer
            x, (w_in, w_out) = ffn(x, layer, 0, mixer_casts=[(conv_w_in, (a,), None), (conv_w_out, (a,), None)])
            x = _conv_mixer(x, mix_norm_pre[layer], w_in, conv_b_in[a], conv_w_dw[a], conv_b_dw[a],
                            conv_ln_g[a], conv_ln_b[a], w_out, conv_b_out[a], mix_norm_post[layer])
            x, _ = ffn(x, layer, 1)
        else:
            j = layer - n_conv
            x, (wq, wk, wv, wo) = ffn(x, layer, 0, mixer_casts=[(attn_w_q, (j,), None), (w_kv, (), (0, 2)),
                                                                (w_kv, (), (1, 2)), (attn_w_o, (j,), None)])
            qt, k, vt, km = _qkv(x, mix_norm_pre[layer], kv_norm, wq, wk, wv)
            att = _attention(qt, k, vt, km)
            x, _ = ffn(x, layer, 1, (att.reshape(b * t, d), wo, mix_norm_post[layer]))
    return x
```

```python
import functools

import numpy as np
import jax
import jax.numpy as jnp
from jax import lax
from jax.experimental import pallas as pl
from jax.experimental.pallas import tpu as pltpu

F32 = jnp.float32
BF16 = jnp.bfloat16

RMS_EPS = 1e-6
LN_EPS = 1e-5
MACARON_WEIGHT = 0.5
NEG_INF = -1e30
LOG2_E = 1.4426950408889634
HEAD_DIM = 128
MOBA_BLOCK = 256
MOBA_TOPK = 3

V7X_VMEM_BYTES = 64 * 1024 * 1024
VMEM_LIMIT_BYTES = V7X_VMEM_BYTES - 8 * 1024 * 1024
LANES = 128
SUBLANES = 8
MXU_COLS = 256
BF16_SUBLANES = 16
CONV_HALO = 32

ROW_TILE = 512
WIDE_ROW_TILE = 1024
FF_CHUNK = 256
STAGE_SLABS = 16
STAGE_SLOTS = 4
HEADS_PER_STEP = 8


_NT = (((1,), (1,)), ((), ()))
_TN_T = (((0,), (1,)), ((), ()))


def _rms(x, g):
    return x * lax.rsqrt(jnp.mean(x * x, axis=-1, keepdims=True) + RMS_EPS) * g


def _resident(shape):
    return pl.BlockSpec(shape, lambda *_: (0,) * len(shape), pipeline_mode=pl.Buffered(1))


def _params(semantics):
    return pltpu.CompilerParams(dimension_semantics=semantics, vmem_limit_bytes=VMEM_LIMIT_BYTES)


def _stage_weight(src_hbm, lead, dst, stage, sem):
    n_slots, slab = stage.shape[:2]
    n_slabs = dst.shape[0] // slab
    ahead = n_slots - 1

    def copy(c):
        slot = c % n_slots
        return pltpu.make_async_copy(src_hbm.at[(*lead, pl.ds(c * slab, slab))], stage.at[slot], sem.at[slot])

    for c in range(min(ahead, n_slabs)):
        copy(c).start()
    for c in range(n_slabs):
        if c + ahead < n_slabs:
            copy(c + ahead).start()
        copy(c).wait()
        dst[c * slab:(c + 1) * slab, :] = stage[c % n_slots].astype(BF16)


def _ffn_kernel(*refs, d_ff, fc, sub, with_proj, n_cast, staged):
    refs = list(refs)
    x_ref = refs.pop(0)
    att_ref, wo_ref, gmix_ref = [refs.pop(0) for _ in range(3)] if with_proj else (None, None, None)
    gpre_ref, wgu_ref, wd_ref, gpost_ref = [refs.pop(0) for _ in range(4)]
    cast_in = [refs.pop(0) for _ in range(n_cast)]
    o_ref = refs.pop(0)
    if staged is not None:
        wgu_hbm, wd_hbm = wgu_ref, wd_ref
        sem, stage_d, stage_gu, wd_ref, wgu_ref = [refs.pop() for _ in range(5)]

        @pl.when(pl.program_id(0) == 0)
        def _():
            _stage_weight(wgu_hbm, staged, wgu_ref, stage_gu, sem)
            _stage_weight(wd_hbm, staged, wd_ref, stage_d, sem)

    for src_ref, dst_ref in zip(cast_in, refs):
        dst_ref[...] = src_ref[...].astype(BF16)
    tiles = []
    for r0 in range(0, x_ref.shape[0], sub):
        x = x_ref[r0:r0 + sub, :]
        if with_proj:
            y = jnp.dot(att_ref[r0:r0 + sub, :], wo_ref[...], preferred_element_type=F32)
            x = x + _rms(y, gmix_ref[...])
        tiles.append((r0, x, _rms(x, gpre_ref[...]).astype(BF16)))
    for r0, x, xn in tiles:
        acc = jnp.zeros(x.shape, F32)
        for f in range(d_ff // fc):
            gate = jnp.dot(xn, wgu_ref[:, f * fc:(f + 1) * fc], preferred_element_type=F32)
            up = jnp.dot(xn, wgu_ref[:, d_ff + f * fc:d_ff + (f + 1) * fc], preferred_element_type=F32)
            h = (gate * jax.nn.sigmoid(gate) * up).astype(BF16)
            acc = acc + jnp.dot(h, wd_ref[f * fc:(f + 1) * fc, :], preferred_element_type=F32)
        o_ref[r0:r0 + sub, :] = x + MACARON_WEIGHT * _rms(acc, gpost_ref[...])


def _ffn(x2d, g_pre, wgu, wd, g_post, proj=None, casts=(), staged=None):
    n, d = x2d.shape
    d_ff = wd.shape[-2]
    tm = WIDE_ROW_TILE
    steps = n // tm
    assert n % tm == 0 and tm % ROW_TILE == 0 and d_ff % FF_CHUNK == 0
    row = pl.BlockSpec((tm, d), lambda i: (i, 0))
    proj_specs, proj_args = [], []
    if proj is not None:
        att2d, wo, g_mix = proj
        proj_specs = [row, _resident((d, d)), _resident((1, d))]
        proj_args = [att2d, wo, g_mix.reshape(1, d)]
    cast_specs, cast_out_specs, cast_out_shapes = [], [], []
    for w, lead, col_block in casts:
        rows, cols = w.shape[-2:]
        cb, n_cb = col_block if col_block is not None else (0, 1)
        assert rows % (steps * BF16_SUBLANES) == 0 and cols % n_cb == 0
        slab = (rows // steps, cols // n_cb)
        cast_specs.append(pl.BlockSpec((None,) * len(lead) + slab, lambda i, lead=lead, cb=cb: (*lead, i, cb)))
        cast_out_specs.append(pl.BlockSpec(slab, lambda i: (i, 0)))
        cast_out_shapes.append(jax.ShapeDtypeStruct((rows, cols // n_cb), BF16))
    if staged is None:
        weight_specs, scratch = [_resident((d, 2 * d_ff)), _resident((d_ff, d))], []
    else:
        assert d % STAGE_SLABS == 0 and d_ff % STAGE_SLABS == 0
        weight_specs = [pl.BlockSpec(memory_space=pl.ANY)] * 2
        scratch = [pltpu.VMEM((d, 2 * d_ff), BF16), pltpu.VMEM((d_ff, d), BF16),
                   pltpu.VMEM((STAGE_SLOTS, d // STAGE_SLABS, 2 * d_ff), F32),
                   pltpu.VMEM((STAGE_SLOTS, d_ff // STAGE_SLABS, d), F32), pltpu.SemaphoreType.DMA((STAGE_SLOTS,))]
    outs = pl.pallas_call(
        functools.partial(_ffn_kernel, d_ff=d_ff, fc=FF_CHUNK, sub=ROW_TILE, with_proj=proj is not None,
                          n_cast=len(casts), staged=staged),
        grid=(steps,),
        in_specs=[row] + proj_specs + [_resident((1, d))] + weight_specs + [_resident((1, d))] + cast_specs,
        out_specs=[row] + cast_out_specs,
        out_shape=[jax.ShapeDtypeStruct((n, d), F32)] + cast_out_shapes,
        scratch_shapes=scratch,
        compiler_params=_params(("arbitrary",) if staged is not None else ("parallel",)),
        name="swiglu_ffn",
    )(x2d, *proj_args, g_pre.reshape(1, d), wgu, wd, g_post.reshape(1, d), *[w for w, _, _ in casts])
    return outs[0], list(outs[1:])


def _conv_kernel(x_ref, gpre_ref, win_ref, bin_ref, wdw_ref, bdw_ref, lng_ref, lnb_ref, wout_ref, bout_ref,
                 gpost_ref, o_ref, ubuf, cbuf, *, tm, kw):
    d = x_ref.shape[-1]
    row_chunk = 128

    @pl.when(pl.program_id(1) == 0)
    def _():
        ubuf[0:CONV_HALO, :] = jnp.zeros((CONV_HALO, d), F32)

    x = x_ref[0]
    h = _rms(x, gpre_ref[...]).astype(BF16)

    first = CONV_HALO - (kw - 1)
    taps_by_shift = {}
    for j in range(kw):
        shift = (first + j) % SUBLANES
        taps_by_shift.setdefault(shift, []).append((j, first + j - shift))

    for c0 in range(0, d, MXU_COLS):
        cols = slice(c0, c0 + MXU_COLS)
        gcols = slice(d + c0, d + c0 + MXU_COLS)
        a = jnp.dot(h, win_ref[:, cols], preferred_element_type=F32) + bin_ref[:, cols]
        gt = jnp.dot(h, win_ref[:, gcols], preferred_element_type=F32) + bin_ref[:, gcols]
        ubuf[CONV_HALO:CONV_HALO + tm, cols] = a * jax.nn.sigmoid(gt)
        for c in range(c0, c0 + MXU_COLS, LANES):
            cs = slice(c, c + LANES)
            for r0 in range(0, tm, row_chunk):
                win = ubuf[r0:r0 + row_chunk + CONV_HALO, cs]
                acc = jnp.broadcast_to(bdw_ref[:, cs], (row_chunk, LANES))
                for shift, taps in sorted(taps_by_shift.items()):
                    rows = row_chunk + (SUBLANES if shift else 0)
                    part = None
                    for j, start in taps:
                        term = wdw_ref[j:j + 1, cs] * win[start:start + rows]
                        part = term if part is None else part + term
                    acc = acc + part[shift:shift + row_chunk]
                cbuf[r0:r0 + row_chunk, cs] = acc
        ubuf[0:CONV_HALO, cols] = ubuf[tm:tm + CONV_HALO, cols]

    cv = cbuf[...]
    mu = jnp.mean(cv, axis=-1, keepdims=True)
    cen = cv - mu
    var = jnp.mean(cen * cen, axis=-1, keepdims=True)
    un = cen * lax.rsqrt(var + LN_EPS) * lng_ref[...] + lnb_ref[...]
    s = (un * jax.nn.sigmoid(un)).astype(BF16)
    y = jnp.dot(s, wout_ref[...], preferred_element_type=F32) + bout_ref[...]
    o_ref[0] = x + _rms(y, gpost_ref[...])


def _conv_mixer(x, g_pre, w_in, b_in, w_dw, b_dw, ln_g, ln_b, w_out, b_out, g_post):
    b, t, d = x.shape
    kw = w_dw.shape[0]
    tm = ROW_TILE
    assert t % tm == 0 and kw - 1 <= CONV_HALO <= tm and d % LANES == 0
    row = pl.BlockSpec((1, tm, d), lambda bi, ti: (bi, ti, 0))
    vec = lambda a: a.reshape(1, -1)
    return pl.pallas_call(
        functools.partial(_conv_kernel, tm=tm, kw=kw),
        grid=(b, t // tm),
        in_specs=[row, _resident((1, d)), _resident((d, 2 * d)), _resident((1, 2 * d)), _resident((kw, d)),
                  _resident((1, d)), _resident((1, d)), _resident((1, d)), _resident((d, d)), _resident((1, d)),
                  _resident((1, d))],
        out_specs=row,
        out_shape=jax.ShapeDtypeStruct((b, t, d), F32),
        scratch_shapes=[pltpu.VMEM((CONV_HALO + tm, d), F32), pltpu.VMEM((tm, d), F32)],
        compiler_params=_params(("arbitrary", "arbitrary")),
        name="conformer_conv",
    )(x, vec(g_pre), w_in, vec(b_in), w_dw, vec(b_dw), vec(ln_g), vec(ln_b), w_out, vec(b_out), vec(g_post))


def _qkv_kernel(x_ref, gmix_ref, gkv_ref, wq_ref, wk_ref, wv_ref, qt_ref, k_ref, vt_ref, km_ref, *, scale):
    x = x_ref[0]
    tm, d = x.shape
    r = x * lax.rsqrt(jnp.mean(x * x, axis=-1, keepdims=True) + RMS_EPS)
    hq = (r * gmix_ref[...]).astype(BF16)
    hkv = (r * gkv_ref[...]).astype(BF16)
    qt_ref[0] = (lax.dot_general(wq_ref[...], hq, _TN_T, preferred_element_type=F32) * scale).astype(BF16)
    k = jnp.dot(hkv, wk_ref[...], preferred_element_type=F32)
    k_ref[0] = k.astype(BF16)
    km_ref[0, 0] = jnp.mean(k.reshape(tm // MOBA_BLOCK, MOBA_BLOCK, d), axis=1)
    vt_ref[0] = lax.dot_general(wv_ref[...], hkv, _TN_T, preferred_element_type=F32).astype(BF16)


def _qkv(x, g_mix, g_kv, wq, wk, wv):
    b, t, d = x.shape
    tm = WIDE_ROW_TILE
    assert t % tm == 0 and tm % MOBA_BLOCK == 0
    per = tm // MOBA_BLOCK
    row = pl.BlockSpec((1, tm, d), lambda bi, ti: (bi, ti, 0))
    col = pl.BlockSpec((1, d, tm), lambda bi, ti: (bi, 0, ti))
    qt, k, vt, km = pl.pallas_call(
        functools.partial(_qkv_kernel, scale=HEAD_DIM ** -0.5 * LOG2_E),
        grid=(b, t // tm),
        in_specs=[row, _resident((1, d)), _resident((1, d)), _resident((d, d)), _resident((d, d)),
                  _resident((d, d))],
        out_specs=[col, row, col, pl.BlockSpec((1, 1, per, d), lambda bi, ti: (bi, ti, 0, 0))],
        out_shape=[jax.ShapeDtypeStruct((b, d, t), BF16), jax.ShapeDtypeStruct((b, t, d), BF16),
                   jax.ShapeDtypeStruct((b, d, t), BF16), jax.ShapeDtypeStruct((b, t // tm, per, d), F32)],
        compiler_params=_params(("parallel", "parallel")),
        name="qkv_proj",
    )(x, g_mix.reshape(1, d), g_kv.reshape(1, d), wq, wk, wv)
    return qt, k, vt, km.reshape(b, t // MOBA_BLOCK, d)


def _select_topk(gate, n_idx):
    picked = jnp.zeros(gate.shape, jnp.bool_)
    for _ in range(MOBA_TOPK):
        best = jnp.max(gate, axis=0, keepdims=True)
        first = jnp.min(jnp.where(gate == best, n_idx, gate.shape[0]), axis=0, keepdims=True)
        hit = n_idx == first
        picked = picked | hit
        gate = jnp.where(hit, -jnp.inf, gate)
    return picked


def _attn_kernel(slopes_ref, qt_ref, k_ref, vt_ref, km_ref, sd_ref, own_ref, o_ref, m_s, l_s, mx_s, acc_s, sel_s,
                 sc_s, *, g):
    blk, dh = MOBA_BLOCK, HEAD_DIM
    hg = pl.program_id(1)
    i = pl.program_id(2)
    nb = km_ref.shape[1]
    n_idx = lax.broadcasted_iota(jnp.int32, (nb, blk), 0)
    past = n_idx < i
    ones_rows = jnp.ones((BF16_SUBLANES, blk), BF16)

    heads = [slice(hh * dh, (hh + 1) * dh) for hh in range(g)]
    slopes = [slopes_ref[hg * g + hh] for hh in range(g)]

    def block_rows(n):
        return pl.ds(pl.multiple_of(n * blk, blk), blk)

    def scores_of(rows):
        return [jnp.dot(k_ref[0, rows, hs], qt_ref[0, hs, :], preferred_element_type=F32)
                for hs in heads]

    def values_of(rows):
        return [jnp.concatenate([vt_ref[0, hs, rows], ones_rows], axis=0) for hs in heads]

    own_scores = scores_of(block_rows(i))
    own_values = values_of(block_rows(i))
    first_scores = scores_of(block_rows(0))
    gates = [jnp.dot(km_ref[0, :, hs].astype(BF16), qt_ref[0, hs, :], preferred_element_type=F32)
             for hs in heads]
    init = []
    for hh in range(g):
        picked = _select_topk(jnp.where(past, gates[hh], NEG_INF), n_idx)
        sel = jnp.where(picked & past, 0.0, NEG_INF)
        s = own_scores[hh] - own_ref[hh]
        m0 = jnp.max(s, axis=0, keepdims=True)
        pv = jnp.dot(own_values[hh], jnp.exp2(s - m0).astype(BF16), preferred_element_type=F32)
        u = first_scores[hh] - sd_ref[hh]
        init.append((sel, m0, pv[dh:dh + 1], pv[:dh], u, jnp.max(u, axis=0, keepdims=True)))
    for hh in range(g):
        sel_s[hh], m_s[hh], l_s[hh], acc_s[hh], sc_s[hh], mx_s[hh] = init[hh]

    def past_block(n, carry):
        base = ((i - n) * blk).astype(F32)
        values = values_of(block_rows(n))
        next_scores = scores_of(block_rows(jnp.minimum(n + 1, i - 1)))
        new = []
        for hh in range(g):
            qbias = sel_s[hh, pl.ds(n, 1), :] - slopes[hh] * base
            m_old = m_s[hh]
            m_new = jnp.maximum(m_old, mx_s[hh] + qbias)
            alpha = jnp.exp2(m_old - m_new)
            p = jnp.exp2(sc_s[hh] - (m_new - qbias))
            pv = jnp.dot(values[hh], p.astype(BF16), preferred_element_type=F32)
            u = next_scores[hh] - sd_ref[hh]
            new.append((m_new, alpha * l_s[hh] + pv[dh:dh + 1], alpha * acc_s[hh] + pv[:dh], u,
                        jnp.max(u, axis=0, keepdims=True)))
        for hh in range(g):
            m_s[hh], l_s[hh], acc_s[hh], sc_s[hh], mx_s[hh] = new[hh]
        return carry

    lax.fori_loop(0, i, past_block, 0)

    for hh in range(g):
        o_ref[0, :, hh * dh:(hh + 1) * dh] = (acc_s[hh] / l_s[hh]).T.astype(BF16)


def _alibi_tables(n_heads):
    slopes = 2.0 ** (-8.0 * (np.arange(n_heads) + 1) / n_heads) * LOG2_E
    pos = np.arange(MOBA_BLOCK)
    dist = (pos[None, :] - pos[:, None]).astype(np.float64)
    sd = slopes[:, None, None] * dist
    own = np.where(dist >= 0, sd, -NEG_INF)
    return jnp.asarray(slopes, F32), jnp.asarray(sd, F32), jnp.asarray(own, F32)


def _attention(qt, k, vt, km):
    b, t, d = k.shape
    g, blk = HEADS_PER_STEP, MOBA_BLOCK
    nb = t // blk
    gw = g * HEAD_DIM
    assert t % blk == 0 and d % gw == 0
    slopes, sd, own = _alibi_tables(d // HEAD_DIM)
    qo = pl.BlockSpec((1, blk, gw), lambda bi, hi, ti: (bi, ti, hi))
    table = pl.BlockSpec((g, blk, blk), lambda bi, hi, ti: (hi, 0, 0))
    stat = pltpu.VMEM((g, 1, blk), F32)
    return pl.pallas_call(
        functools.partial(_attn_kernel, g=g),
        grid=(b, d // gw, nb),
        in_specs=[pl.BlockSpec(memory_space=pltpu.SMEM),
                  pl.BlockSpec((1, gw, blk), lambda bi, hi, ti: (bi, hi, ti)),
                  pl.BlockSpec((1, t, gw), lambda bi, hi, ti: (bi, 0, hi)),
                  pl.BlockSpec((1, gw, t), lambda bi, hi, ti: (bi, hi, 0)),
                  pl.BlockSpec((1, nb, gw), lambda bi, hi, ti: (bi, 0, hi)), table, table],
        out_specs=qo,
        out_shape=jax.ShapeDtypeStruct((b, t, d), BF16),
        scratch_shapes=[stat, stat, stat, pltpu.VMEM((g, HEAD_DIM, blk), F32), pltpu.VMEM((g, nb, blk), F32),
                        pltpu.VMEM((g, blk, blk), F32)],
        compiler_params=_params(("parallel", "parallel", "arbitrary")),
        name="moba_attention",
    )(slopes, qt, k, vt, km, sd, own)


def kernel(x, ffn_norm_pre, ffn_norm_post, ffn_w_gate_up, ffn_w_down, mix_norm_pre, mix_norm_post, conv_w_in,
           conv_b_in, conv_w_dw, conv_b_dw, conv_ln_g, conv_ln_b, conv_w_out, conv_b_out, kv_norm, w_kv,
           attn_w_q, attn_w_o):
    b, t, d = x.shape
    depth = ffn_w_gate_up.shape[0]
    n_conv = conv_w_in.shape[0]
    assert depth - n_conv == attn_w_q.shape[0] == 1, "one MoBA layer reading the shared K/V is supported"
    halves = [(layer, half) for layer in range(depth) for half in range(2)]
    weights = None

    def ffn(x, layer, half, proj=None, mixer_casts=()):
        nonlocal weights
        nxt = halves.index((layer, half)) + 1
        casts = [(w, halves[nxt], None) for w in (ffn_w_gate_up, ffn_w_down)] if nxt < len(halves) else []
        own = (ffn_w_gate_up, ffn_w_down) if weights is None else weights
        y, casted = _ffn(x.reshape(b * t, d), ffn_norm_pre[layer, half], *own, ffn_norm_post[layer, half], proj,
                         casts + list(mixer_casts), staged=(layer, half) if weights is None else None)
        weights = casted[:len(casts)]
        return y.reshape(b, t, d), casted[len(casts):]

    for layer in range(depth):
        if layer < n_conv:
            a = layer
            x, (w_in, w_out) = ffn(x, layer, 0, mixer_casts=[(conv_w_in, (a,), None), (conv_w_out, (a,), None)])
            x = _conv_mixer(x, mix_norm_pre[layer], w_in, conv_b_in[a], conv_w_dw[a], conv_b_dw[a],
                            conv_ln_g[a], conv_ln_b[a], w_out, conv_b_out[a], mix_norm_post[layer])
            x, _ = ffn(x, layer, 1)
        else:
            j = layer - n_conv
            x, (wq, wk, wv, wo) = ffn(x, layer, 0, mixer_casts=[(attn_w_q, (j,), None), (w_kv, (), (0, 2)),
                                                                (w_kv, (), (1, 2)), (attn_w_o, (j,), None)])
            qt, k, vt, km = _qkv(x, mix_norm_pre[layer], kv_norm, wq, wk, wv)
            att = _attention(qt, k, vt, km)
            x, _ = ffn(x, layer, 1, (att.reshape(b * t, d), wo, mix_norm_post[layer]))
    return x
```
